```python
import math
import jax, jax.numpy as jnp
from jax import lax
import numpy as np

D_MODEL = 1024
BATCH = 16
SEQ = 2048
DEPTH = 1

CHUNK = 64
RET_HEADS = 4
RET_KEY_DIM = 128
RET_VAL_DIM = 256
RET_QK = RET_HEADS * RET_KEY_DIM
RET_V = RET_HEADS * RET_VAL_DIM
ATT_HEADS = 8
ATT_HEAD_DIM = 64
ATT_W = ATT_HEADS * ATT_HEAD_DIM
BAND_CHUNKS = 8
BAND = (BAND_CHUNKS + 1) * CHUNK
MAX_REL = 256
N_REL = CHUNK + MAX_REL
D_FF = -(-8 * D_MODEL // (3 * 256)) * 256
ROPE_BASE = 10000.0
EPS = 1e-6
NEG_INF = -1e30
IN_SIZES = (RET_QK, RET_QK, RET_V, RET_V, ATT_W, ATT_W, ATT_W, 2 * D_MODEL)
IN_SPLITS = tuple(int(s) for s in np.cumsum(IN_SIZES)[:-1])
N_IN = int(sum(IN_SIZES))

kernel_name = "hybrid_retention_chunkattn_gated_block"


def rmsnorm(x, g):
    xf = x.astype(jnp.float32)
    y = xf * lax.rsqrt(jnp.mean(xf * xf, axis=-1, keepdims=True) + EPS)
    return (y * g.astype(jnp.float32)).astype(x.dtype)


def rotary(x, pos):
    d = x.shape[-1]
    freqs = ROPE_BASE ** (-jnp.arange(0, d, 2, dtype=jnp.float32) / d)
    ang = pos[:, None] * freqs[None, :]
    cos = jnp.cos(ang)[None, :, None, :].astype(x.dtype)
    sin = jnp.sin(ang)[None, :, None, :].astype(x.dtype)
    x1, x2 = x[..., : d // 2], x[..., d // 2:]
    return jnp.concatenate([x1 * cos - x2 * sin, x1 * sin + x2 * cos], axis=-1)


def retention(q, k, v):
    B, S, H, dk = q.shape
    dv = v.shape[-1]
    nc = S // CHUNK
    dt = q.dtype
    log_g = jnp.log(1.0 - 2.0 ** (-5.0 - jnp.arange(H, dtype=jnp.float32)))
    p = jnp.arange(CHUNK, dtype=jnp.float32)
    intra = jnp.exp(log_g[:, None, None] * jnp.abs(p[:, None] - p[None, :])).astype(dt)
    q_dec = jnp.exp(log_g[None, :] * (p[:, None] + 1.0)).astype(dt)[None, :, :, None]
    k_dec = jnp.exp(log_g[None, :] * (CHUNK - 1.0 - p[:, None])).astype(dt)[None, :, :, None]
    chunk_dec = jnp.exp(log_g * CHUNK).astype(dt)[None, :, None, None]
    k = k * jnp.asarray(RET_KEY_DIM ** -0.5, dt)

    def to_chunks(t):
        return t.reshape(B, nc, CHUNK, H, t.shape[-1]).transpose(1, 0, 2, 3, 4)

    def step(state, inp):
        qi, ki, vi = inp
        s = jnp.einsum('bnhd,bmhd->bhnm', qi, ki) * intra[None]
        o = jnp.einsum('bhnm,bmhe->bnhe', s, vi)
        o = o + jnp.einsum('bnhd,bhde->bnhe', qi * q_dec, state)
        new_state = state * chunk_dec + jnp.einsum('bmhd,bmhe->bhde', ki * k_dec, vi)
        return new_state, o

    state0 = jnp.zeros((B, H, dk, dv), dt)
    _, o = lax.scan(step, state0, (to_chunks(q), to_chunks(k), to_chunks(v)))
    return o.transpose(1, 0, 2, 3, 4).reshape(B, S, H, dv)


def head_groupnorm(o):
    of = o.astype(jnp.float32)
    mu = jnp.mean(of, axis=-1, keepdims=True)
    var = jnp.mean(jnp.square(of - mu), axis=-1, keepdims=True)
    return ((of - mu) * lax.rsqrt(var + EPS)).astype(o.dtype)


def chunk_band_attention(q, k, v, rel_bias):
    B, S, H, dh = q.shape
    nc = S // CHUNK
    pad = BAND_CHUNKS * CHUNK
    kp = jnp.pad(k, ((0, 0), (pad, 0), (0, 0), (0, 0)))
    vp = jnp.pad(v, ((0, 0), (pad, 0), (0, 0), (0, 0)))
    n = jnp.arange(CHUNK)
    j = jnp.arange(BAND)
    rel = (pad + n[:, None]) - j[None, :]
    idx = jnp.clip(rel, -(CHUNK - 1), MAX_REL) + (CHUNK - 1)
    bias = rel_bias.astype(jnp.float32)[:, idx]
    scale = ATT_HEAD_DIM ** -0.5
    qc = q.reshape(B, nc, CHUNK, H, dh).transpose(1, 0, 2, 3, 4)

    def one_chunk(args):
        i, qi = args
        ki = lax.dynamic_slice_in_dim(kp, i * CHUNK, BAND, axis=1)
        vi = lax.dynamic_slice_in_dim(vp, i * CHUNK, BAND, axis=1)
        s = jnp.einsum('bnhd,bmhd->bhnm', qi, ki).astype(jnp.float32) * scale + bias[None]
        valid = j >= (BAND_CHUNKS - i) * CHUNK
        s = jnp.where(valid[None, None, None, :], s, NEG_INF)
        pr = jax.nn.softmax(s, axis=-1).astype(vi.dtype)
        return jnp.einsum('bhnm,bmhd->bnhd', pr, vi)

    o = lax.map(one_chunk, (jnp.arange(nc), qc))
    return o.transpose(1, 0, 2, 3, 4).reshape(B, S, H * dh)


def setup_inputs(seed: int = 0) -> dict:
    key = jax.random.key(seed)
    ks = jax.random.split(key, 16)
    f32 = jnp.float32

    def w(k, shape, fan_in):
        return jax.random.normal(k, shape, f32) * (fan_in ** -0.5)

    return {
        "x": jax.random.normal(ks[0], (BATCH, SEQ, D_MODEL), f32),
        "norm_mix": 1.0 + 0.05 * jax.random.normal(ks[1], (DEPTH, D_MODEL), f32),
        "w_in": w(ks[2], (DEPTH, D_MODEL, N_IN), D_MODEL),
        "b_gate": 0.02 * jax.random.normal(ks[3], (DEPTH, 2 * D_MODEL), f32),
        "rel_bias": 0.1 * jax.random.normal(ks[4], (DEPTH, ATT_HEADS, N_REL), f32),
        "w_ret_out": w(ks[5], (DEPTH, RET_V, D_MODEL), RET_V),
        "w_att_out": w(ks[6], (DEPTH, ATT_W, D_MODEL), ATT_W),
        "w_out": w(ks[7], (DEPTH, D_MODEL, D_MODEL), D_MODEL),
        "norm_ffn": 1.0 + 0.05 * jax.random.normal(ks[8], (DEPTH, D_MODEL), f32),
        "w_ffn_gate": w(ks[9], (DEPTH, D_MODEL, D_FF), D_MODEL),
        "w_ffn_up": w(ks[10], (DEPTH, D_MODEL, D_FF), D_MODEL),
        "w_ffn_down": w(ks[11], (DEPTH, D_FF, D_MODEL), D_FF),
        "norm_final": 1.0 + 0.05 * jax.random.normal(ks[12], (D_MODEL,), f32),
    }


def reference(x, norm_mix, w_in, b_gate, rel_bias, w_ret_out, w_att_out, w_out,
              norm_ffn, w_ffn_gate, w_ffn_up, w_ffn_down, norm_final):
    B, S, _ = x.shape
    pos = jnp.arange(S, dtype=jnp.float32)
    h = x
    for l in range(DEPTH):
        xn = rmsnorm(h, norm_mix[l])
        proj = xn @ w_in[l]
        rq, rk, rv, rg, aq, ak, av, gl = jnp.split(proj, IN_SPLITS, axis=-1)
        rq = rotary(rq.reshape(B, S, RET_HEADS, RET_KEY_DIM), pos)
        rk = rotary(rk.reshape(B, S, RET_HEADS, RET_KEY_DIM), pos)
        rv = rv.reshape(B, S, RET_HEADS, RET_VAL_DIM)
        ro = head_groupnorm(retention(rq, rk, rv)).reshape(B, S, RET_V)
        y_ret = (jax.nn.silu(rg) * ro) @ w_ret_out[l]
        ao = chunk_band_attention(aq.reshape(B, S, ATT_HEADS, ATT_HEAD_DIM),
                                  ak.reshape(B, S, ATT_HEADS, ATT_HEAD_DIM),
                                  av.reshape(B, S, ATT_HEADS, ATT_HEAD_DIM),
                                  rel_bias[l])
        y_att = ao @ w_att_out[l]
        gates = jax.nn.sigmoid(gl + b_gate[l])
        g_ret, g_att = gates[..., :D_MODEL], gates[..., D_MODEL:]
        h = h + (g_ret * y_ret + g_att * y_att) @ w_out[l]
        hn = rmsnorm(h, norm_ffn[l])
        h = h + (jax.nn.silu(hn @ w_ffn_gate[l]) * (hn @ w_ffn_up[l])) @ w_ffn_down[l]
    return rmsnorm(h, norm_final)
```

```python
import jax
import jax.numpy as jnp
import numpy as np
from jax import lax
from jax.experimental import pallas as pl
from jax.experimental.pallas import tpu as pltpu

D_MODEL = 1024
CHUNK = 64
RET_HEADS = 4
RET_KEY_DIM = 128
RET_VAL_DIM = 256
RET_QK = RET_HEADS * RET_KEY_DIM
RET_V = RET_HEADS * RET_VAL_DIM
ATT_HEADS = 8
ATT_HEAD_DIM = 64
ATT_W = ATT_HEADS * ATT_HEAD_DIM
BAND_CHUNKS = 8
MAX_REL = 256
ROPE_BASE = 10000.0
EPS = 1e-6
NEG_INF = -1e30
F32_MAX = float(np.finfo(np.float32).max)

V7X_LANES = 128
V7X_VMEM_BYTES = 64 * 1024 * 1024
VMEM_LIMIT_BYTES = 56 * 1024 * 1024

ROW_TILE = 512
COL_CHUNK = 512
Q_BLOCK = 2 * CHUNK
KEY_WINDOW = Q_BLOCK + BAND_CHUNKS * CHUNK
PAD_ROWS = BAND_CHUNKS * CHUNK

BF16 = jnp.bfloat16
F32 = jnp.float32


def _const_spec(shape):
    zeros = (0,) * len(shape)
    return pl.BlockSpec(shape, lambda *_: zeros, pipeline_mode=pl.Buffered(1))


def _dot(a, b):
    return jnp.dot(a, b, preferred_element_type=F32)


def _dot_nt(a, b):
    return lax.dot_general(a, b, (((1,), (1,)), ((), ())), preferred_element_type=F32)


def _dot_tn(a, b):
    return lax.dot_general(a, b, (((0,), (0,)), ((), ())), preferred_element_type=F32)


def _sigmoid(x):
    return 1.0 / (1.0 + jnp.exp(-x))


def _rms_scale(x):
    return lax.rsqrt(jnp.mean(x * x, axis=-1, keepdims=True) + EPS)


def _in_proj_kernel(x_ref, g_ref, w_ref, cos_ref, sin_ref,
                    rq_ref, rk_ref, rv_ref, rg_ref, aq_ref, ak_ref, av_ref, gl_ref):
    x = x_ref[...]
    xn = ((x * _rms_scale(x)) * g_ref[...]).astype(BF16)
    cos = cos_ref[...]
    sin = sin_ref[...]

    def proj(col0, width):
        return _dot(xn, w_ref[:, col0:col0 + width])

    def rotary_store(out_ref, col0, scale):
        t = proj(col0, RET_QK)
        for h in range(RET_HEADS):
            cols = slice(h * RET_KEY_DIM, (h + 1) * RET_KEY_DIM)
            th = t[:, cols]
            rot = th * cos + pltpu.roll(th, RET_KEY_DIM // 2, axis=1) * sin
            out_ref[:, cols] = (rot if scale is None else rot * scale).astype(BF16)

    rotary_store(rq_ref, 0, None)
    rotary_store(rk_ref, RET_QK, RET_KEY_DIM ** -0.5)
    col = 2 * RET_QK
    for out_ref, width, scale in ((rv_ref, RET_V, None), (rg_ref, RET_V, None),
                                  (aq_ref, ATT_W, ATT_HEAD_DIM ** -0.5), (ak_ref, ATT_W, None),
                                  (av_ref, ATT_W, None), (gl_ref, 2 * D_MODEL, None)):
        for c in range(0, width, COL_CHUNK):
            t = proj(col + c, COL_CHUNK)
            if scale is not None:
                t = t * scale
            out_ref[:, c:c + COL_CHUNK] = t.astype(BF16)
        col += width


def _in_proj(x2d, g, w_in, cos, sin, seq):
    m = x2d.shape[0]
    n_in = w_in.shape[1]
    steps_per_seq = seq // ROW_TILE
    row = lambda width: pl.BlockSpec((ROW_TILE, width), lambda i: (i, 0))
    pos = pl.BlockSpec((ROW_TILE, RET_KEY_DIM), lambda i: (i % steps_per_seq, 0))
    widths = (RET_QK, RET_QK, RET_V, RET_V, ATT_W, ATT_W, ATT_W, 2 * D_MODEL)
    return pl.pallas_call(
        _in_proj_kernel,
        grid=(m // ROW_TILE,),
        in_specs=[row(D_MODEL), _const_spec((1, D_MODEL)), _const_spec((D_MODEL, n_in)), pos, pos],
        out_specs=[row(w) for w in widths],
        out_shape=[jax.ShapeDtypeStruct((m, w), BF16) for w in widths],
        compiler_params=pltpu.CompilerParams(dimension_semantics=("arbitrary",),
                                             vmem_limit_bytes=VMEM_LIMIT_BYTES),
        name="in_proj",
    )(x2d, g, w_in, cos, sin)


def _mixer_kernel(rq_ref, rk_ref, rv_ref, rg_ref, aq_ref, ak_ref, av_ref,
                  bias_ref, cap_ref, intra_ref, qdec_ref, kdec_ref, cdec_ref,
                  rog_ref, ao_ref, kpad_ref, vpad_ref, state_ref):
    g = pl.program_id(1)

    @pl.when(g == 0)
    def _():
        kpad_ref[0:PAD_ROWS, :] = jnp.zeros((PAD_ROWS, ATT_W), BF16)
        vpad_ref[0:PAD_ROWS, :] = jnp.zeros((PAD_ROWS, ATT_W), BF16)
        kpad_ref[PAD_ROWS:, :] = ak_ref[...]
        vpad_ref[PAD_ROWS:, :] = av_ref[...]
        state_ref[...] = jnp.zeros(state_ref.shape, F32)

    start = pl.multiple_of(g * Q_BLOCK, Q_BLOCK)
    kwin = kpad_ref[pl.ds(start, KEY_WINDOW), :]
    vwin = vpad_ref[pl.ds(start, KEY_WINDOW), :]
    cap = cap_ref[0]
    pair_w = 2 * ATT_HEAD_DIM
    low_half = lax.broadcasted_iota(jnp.int32, (Q_BLOCK, pair_w), 1) < ATT_HEAD_DIM
    for p in range(ATT_HEADS // 2):
        cols = slice(p * pair_w, (p + 1) * pair_w)
        qp = aq_ref[:, cols]
        kp = kwin[:, cols]
        vp = vwin[:, cols]
        outs = []
        for e in range(2):
            qh = jnp.where(low_half if e == 0 else ~low_half, qp, jnp.zeros_like(qp))
            s = _dot_nt(qh, kp)
            s = jnp.minimum(s + bias_ref[2 * p + e], cap)
            s_max = jnp.max(s, axis=-1, keepdims=True)
            prob = jnp.exp(s - s_max)
            denom = jnp.sum(prob, axis=-1, keepdims=True)
            outs.append(_dot(prob.astype(BF16), vp) * (1.0 / denom))
        ao_ref[:, cols] = jnp.where(low_half, outs[0], outs[1]).astype(BF16)

    for c in range(Q_BLOCK // CHUNK):
        rows = slice(c * CHUNK, (c + 1) * CHUNK)
        for h in range(RET_HEADS):
            kcols = slice(h * RET_KEY_DIM, (h + 1) * RET_KEY_DIM)
            vcols = slice(h * RET_VAL_DIM, (h + 1) * RET_VAL_DIM)
            q = rq_ref[rows, kcols]
            k = rk_ref[rows, kcols]
            v = rv_ref[rows, vcols]
            s = _dot_nt(q, k) * intra_ref[h]
            o = _dot(s.astype(BF16), v)
            state = state_ref[h]
            qd = (q.astype(F32) * qdec_ref[:, kcols]).astype(BF16)
            o = o + _dot(qd, state.astype(BF16))
            kd = (k.astype(F32) * kdec_ref[:, kcols]).astype(BF16)
            state_ref[h] = state * cdec_ref[h] + _dot_tn(kd, v)
            mu = jnp.mean(o, axis=-1, keepdims=True)
            dev = o - mu
            var = jnp.mean(dev * dev, axis=-1, keepdims=True)
            ro = dev * lax.rsqrt(var + EPS)
            gate = rg_ref[rows, vcols].astype(F32)
            rog_ref[rows, vcols] = ((gate * _sigmoid(gate)) * ro).astype(BF16)


def _mixers(rq, rk, rv, rg, aq, ak, av, bias, cap, intra, qdec, kdec, cdec, batch, seq):
    m = rq.shape[0]
    blocks_per_seq = seq // Q_BLOCK
    n_cap = cap.shape[0]
    qrow = lambda width: pl.BlockSpec((Q_BLOCK, width), lambda b, g: (b * blocks_per_seq + g, 0))
    seq_rows = pl.BlockSpec((seq, ATT_W), lambda b, g: (b, 0))
    cap_spec = pl.BlockSpec((1, Q_BLOCK, KEY_WINDOW), lambda b, g: (jnp.minimum(g, n_cap - 1), 0, 0))
    return pl.pallas_call(
        _mixer_kernel,
        grid=(batch, blocks_per_seq),
        in_specs=[qrow(RET_QK), qrow(RET_QK), qrow(RET_V), qrow(RET_V), qrow(ATT_W), seq_rows, seq_rows,
                  _const_spec(bias.shape), cap_spec, _const_spec(intra.shape),
                  _const_spec(qdec.shape), _const_spec(kdec.shape), _const_spec(cdec.shape)],
        out_specs=[qrow(RET_V), qrow(ATT_W)],
        out_shape=[jax.ShapeDtypeStruct((m, RET_V), BF16), jax.ShapeDtypeStruct((m, ATT_W), BF16)],
        scratch_shapes=[pltpu.VMEM((PAD_ROWS + seq, ATT_W), BF16),
                        pltpu.VMEM((PAD_ROWS + seq, ATT_W), BF16),
                        pltpu.VMEM((RET_HEADS, RET_KEY_DIM, RET_VAL_DIM), F32)],
        compiler_params=pltpu.CompilerParams(dimension_semantics=("arbitrary", "arbitrary"),
                                             vmem_limit_bytes=VMEM_LIMIT_BYTES),
        name="mixers",
    )(rq, rk, rv, rg, aq, ak, av, bias, cap, intra, qdec, kdec, cdec)


def _merge_kernel(x_ref, rog_ref, ao_ref, gl_ref, bg_ref, wro_ref, wao_ref, wout_ref, h_ref):
    y_ret = _dot(rog_ref[...], wro_ref[...])
    y_att = _dot(ao_ref[...], wao_ref[...])
    gates = _sigmoid(gl_ref[...].astype(F32) + bg_ref[...])
    mixed = gates[:, :D_MODEL] * y_ret + gates[:, D_MODEL:] * y_att
    h_ref[...] = x_ref[...] + _dot(mixed.astype(BF16), wout_ref[...])


def _merge(x2d, rog, ao, gl, b_gate, w_ret_out, w_att_out, w_out):
    m = x2d.shape[0]
    row = lambda width: pl.BlockSpec((ROW_TILE, width), lambda i: (i, 0))
    return pl.pallas_call(
        _merge_kernel,
        grid=(m // ROW_TILE,),
        in_specs=[row(D_MODEL), row(RET_V), row(ATT_W), row(2 * D_MODEL), _const_spec(b_gate.shape),
                  _const_spec(w_ret_out.shape), _const_spec(w_att_out.shape), _const_spec(w_out.shape)],
        out_specs=row(D_MODEL),
        out_shape=jax.ShapeDtypeStruct((m, D_MODEL), F32),
        compiler_params=pltpu.CompilerParams(dimension_semantics=("arbitrary",),
                                             vmem_limit_bytes=VMEM_LIMIT_BYTES),
        name="merge",
    )(x2d, rog, ao, gl, b_gate, w_ret_out, w_att_out, w_out)


def _ffn_kernel(h_ref, gf_ref, wg_ref, wu_ref, wd_ref, gn_ref, o_ref):
    h = h_ref[...]
    hn = ((h * _rms_scale(h)) * gf_ref[...]).astype(BF16)
    gate = _dot(hn, wg_ref[...])
    up = _dot(hn, wu_ref[...])
    act = ((gate * _sigmoid(gate)) * up).astype(BF16)
    h2 = h + _dot(act, wd_ref[...])
    o_ref[...] = (h2 * _rms_scale(h2)) * gn_ref[...]


def _ffn(h1, norm_ffn, w_gate, w_up, w_down, norm_final):
    m = h1.shape[0]
    row = pl.BlockSpec((ROW_TILE, D_MODEL), lambda i: (i, 0))
    return pl.pallas_call(
        _ffn_kernel,
        grid=(m // ROW_TILE,),
        in_specs=[row, _const_spec(norm_ffn.shape), _const_spec(w_gate.shape), _const_spec(w_up.shape),
                  _const_spec(w_down.shape), _const_spec(norm_final.shape)],
        out_specs=row,
        out_shape=jax.ShapeDtypeStruct((m, D_MODEL), F32),
        compiler_params=pltpu.CompilerParams(dimension_semantics=("arbitrary",),
                                             vmem_limit_bytes=VMEM_LIMIT_BYTES),
        name="ffn",
    )(h1, norm_ffn, w_gate, w_up, w_down, norm_final)


def _rotary_tables(seq):
    pos = jnp.arange(seq, dtype=F32)
    d = RET_KEY_DIM
    freqs = ROPE_BASE ** (-jnp.arange(0, d, 2, dtype=F32) / d)
    ang = pos[:, None] * freqs[None, :]
    cos = jnp.cos(ang)
    sin = jnp.sin(ang)
    return jnp.concatenate([cos, cos], axis=-1), jnp.concatenate([-sin, sin], axis=-1)


def _retention_tables():
    log_g = jnp.log(1.0 - 2.0 ** (-5.0 - jnp.arange(RET_HEADS, dtype=F32)))
    p = jnp.arange(CHUNK, dtype=F32)
    intra = jnp.exp(log_g[:, None, None] * jnp.abs(p[:, None] - p[None, :]))
    q_dec = jnp.exp(log_g[None, :] * (p[:, None] + 1.0))
    k_dec = jnp.exp(log_g[None, :] * (CHUNK - 1.0 - p[:, None]))
    chunk_dec = jnp.exp(log_g * CHUNK)
    qdec = jnp.repeat(q_dec, RET_KEY_DIM, axis=1)
    kdec = jnp.repeat(k_dec, RET_KEY_DIM, axis=1)
    cdec = jnp.broadcast_to(chunk_dec[:, None, None], (RET_HEADS, 1, RET_VAL_DIM))
    return intra, qdec, kdec, cdec


def _band_tables(rel_bias):
    n = np.arange(Q_BLOCK)[:, None]
    j = np.arange(KEY_WINDOW)[None, :]
    rel = n + PAD_ROWS - j
    idx = np.clip(rel, -(CHUNK - 1), MAX_REL) + (CHUNK - 1)
    bias = rel_bias.astype(F32)[:, idx]
    in_band = (j // CHUNK >= n // CHUNK) & (j // CHUNK <= n // CHUNK + BAND_CHUNKS)
    n_cap = PAD_ROWS // Q_BLOCK + 1
    caps = []
    for g in range(n_cap):
        visible = in_band & (g * Q_BLOCK - PAD_ROWS + j >= 0)
        caps.append(np.where(visible, F32_MAX, NEG_INF))
    return bias, jnp.asarray(np.stack(caps), F32)


def kernel(x, norm_mix, w_in, b_gate, rel_bias, w_ret_out, w_att_out, w_out, norm_ffn,
           w_ffn_gate, w_ffn_up, w_ffn_down, norm_final):
    batch, seq, d_model = x.shape
    assert w_in.shape[0] == 1, "single-layer block"
    assert d_model == D_MODEL and seq % ROW_TILE == 0 and seq % Q_BLOCK == 0
    assert PAD_ROWS % Q_BLOCK == 0 and w_in.shape[2] % COL_CHUNK == 0
    cos, sin = _rotary_tables(seq)
    intra, qdec, kdec, cdec = _retention_tables()
    bias, cap = _band_tables(rel_bias[0])
    x2d = x.reshape(batch * seq, d_model)
    rq, rk, rv, rg, aq, ak, av, gl = _in_proj(x2d, norm_mix, w_in[0].astype(BF16), cos, sin, seq)
    rog, ao = _mixers(rq, rk, rv, rg, aq, ak, av, bias, cap, intra, qdec, kdec, cdec, batch, seq)
    h1 = _merge(x2d, rog, ao, gl, b_gate, w_ret_out[0].astype(BF16),
                w_att_out[0].astype(BF16), w_out[0].astype(BF16))
    out = _ffn(h1, norm_ffn, w_ffn_gate[0].astype(BF16), w_ffn_up[0].astype(BF16),
               w_ffn_down[0].astype(BF16), norm_final[None, :])
    return out.reshape(batch, seq, d_model)
```

```python
import jax
import jax.numpy as jnp
import numpy as np
from jax import lax
from jax.experimental import pallas as pl
from jax.experimental.pallas import tpu as pltpu

D_MODEL = 1024
CHUNK = 64
RET_HEADS = 4
RET_KEY_DIM = 128
RET_VAL_DIM = 256
RET_QK = RET_HEADS * RET_KEY_DIM
RET_V = RET_HEADS * RET_VAL_DIM
ATT_HEADS = 8
ATT_HEAD_DIM = 64
ATT_W = ATT_HEADS * ATT_HEAD_DIM
BAND_CHUNKS = 8
MAX_REL = 256
N_REL = CHUNK + MAX_REL
ROPE_BASE = 10000.0
EPS = 1e-6
NEG_INF = -1e30
F32_MAX = float(np.finfo(np.float32).max)

V7X_LANES = 128
VMEM_LIMIT_BYTES = 56 * 1024 * 1024

ROW_TILE = 512
COL_CHUNK = 512
Q_BLOCK = 4 * CHUNK
PAD_ROWS = BAND_CHUNKS * CHUNK
KEY_WINDOW = Q_BLOCK + PAD_ROWS
HALF_Q = Q_BLOCK // 2
HALF_KEYS = HALF_Q + PAD_ROWS
BIAS_EXT = 1024
BIAS_ROLL_ROWS = 128

assert HALF_Q == V7X_LANES and KEY_WINDOW - HALF_KEYS == HALF_Q
assert BIAS_EXT >= KEY_WINDOW + Q_BLOCK - 1

BF16 = jnp.bfloat16
F32 = jnp.float32


def _const_spec(shape):
    zeros = (0,) * len(shape)
    return pl.BlockSpec(shape, lambda *_: zeros, pipeline_mode=pl.Buffered(1))


def _dot(a, b):
    return jnp.dot(a, b, preferred_element_type=F32)


def _dot_nt(a, b):
    return lax.dot_general(a, b, (((1,), (1,)), ((), ())), preferred_element_type=F32)


def _dot_tn(a, b):
    return lax.dot_general(a, b, (((0,), (0,)), ((), ())), preferred_element_type=F32)


def _sigmoid(x):
    return 1.0 / (1.0 + jnp.exp(-x))


def _rms_scale(x):
    return lax.rsqrt(jnp.mean(x * x, axis=-1, keepdims=True) + EPS)


def _in_proj_kernel(x_ref, g_ref, w_row_ref, w_t_ref, cos_ref, sin_ref,
                    rq_ref, rk_ref, aq_ref, ak_ref, gl_ref, rvt_ref, rgt_ref, avt_ref):
    x = x_ref[...]
    xn = ((x * _rms_scale(x)) * g_ref[...]).astype(BF16)
    cos = cos_ref[...]
    sin = sin_ref[...]

    def rotary_store(out_ref, col0, scale):
        t = _dot(xn, w_row_ref[:, col0:col0 + RET_QK])
        for h in range(RET_HEADS):
            cols = slice(h * RET_KEY_DIM, (h + 1) * RET_KEY_DIM)
            th = t[:, cols]
            rot = th * cos + pltpu.roll(th, RET_KEY_DIM // 2, axis=1) * sin
            out_ref[:, cols] = (rot if scale is None else rot * scale).astype(BF16)

    rotary_store(rq_ref, 0, None)
    rotary_store(rk_ref, RET_QK, RET_KEY_DIM ** -0.5)
    col = 2 * RET_QK
    for out_ref, width, scale in ((aq_ref, ATT_W, ATT_HEAD_DIM ** -0.5), (ak_ref, ATT_W, None),
                                  (gl_ref, 2 * D_MODEL, None)):
        for c in range(0, width, COL_CHUNK):
            t = _dot(xn, w_row_ref[:, col + c:col + c + COL_CHUNK])
            out_ref[:, c:c + COL_CHUNK] = (t if scale is None else t * scale).astype(BF16)
        col += width
    row = 0
    for out_ref, height in ((rvt_ref, RET_V), (rgt_ref, RET_V), (avt_ref, ATT_W)):
        for r in range(0, height, COL_CHUNK):
            t = _dot_nt(w_t_ref[row + r:row + r + COL_CHUNK, :], xn)
            out_ref[r:r + COL_CHUNK, :] = t.astype(BF16)
        row += height


def _in_proj(x2d, g, w_row, w_t, cos, sin, seq):
    m = x2d.shape[0]
    steps_per_seq = seq // ROW_TILE
    row = lambda width: pl.BlockSpec((ROW_TILE, width), lambda i: (i, 0))
    col = lambda height: pl.BlockSpec((height, ROW_TILE), lambda i: (0, i))
    pos = pl.BlockSpec((ROW_TILE, RET_KEY_DIM), lambda i: (i % steps_per_seq, 0))
    row_widths = (RET_QK, RET_QK, ATT_W, ATT_W, 2 * D_MODEL)
    col_heights = (RET_V, RET_V, ATT_W)
    return pl.pallas_call(
        _in_proj_kernel,
        grid=(m // ROW_TILE,),
        in_specs=[row(D_MODEL), _const_spec(g.shape), _const_spec(w_row.shape), _const_spec(w_t.shape),
                  pos, pos],
        out_specs=[row(w) for w in row_widths] + [col(h) for h in col_heights],
        out_shape=([jax.ShapeDtypeStruct((m, w), BF16) for w in row_widths]
                   + [jax.ShapeDtypeStruct((h, m), BF16) for h in col_heights]),
        compiler_params=pltpu.CompilerParams(dimension_semantics=("arbitrary",),
                                             vmem_limit_bytes=VMEM_LIMIT_BYTES),
        name="in_proj",
    )(x2d, g, w_row, w_t, cos, sin)


def _mixer_kernel(rq_ref, rk_ref, rvt_ref, rgt_ref, aq_ref, ak_ref, avt_ref,
                  ebias_ref, cap_ref, dect_ref, qdec_ref, kdec_ref, cdec_ref,
                  rogt_ref, aot_ref, kpad_ref, vtpad_ref, state_ref, biast_ref):
    b = pl.program_id(0)
    g = pl.program_id(1)
    n_pad_blocks = PAD_ROWS // Q_BLOCK

    @pl.when((b == 0) & (g == 0))
    def _():
        for h in range(ATT_HEADS):
            ext = jnp.broadcast_to(ebias_ref[h:h + 1, :], (BIAS_ROLL_ROWS, BIAS_EXT))
            for r in range(0, KEY_WINDOW, BIAS_ROLL_ROWS):
                rolled = pltpu.roll(ext, BIAS_EXT - KEY_WINDOW + 1 + r, axis=1, stride=1, stride_axis=0)
                biast_ref[h, r:r + BIAS_ROLL_ROWS, :] = rolled[:, :Q_BLOCK]

    @pl.when(g == 0)
    def _():
        kpad_ref[0:PAD_ROWS, :] = jnp.zeros((PAD_ROWS, ATT_W), BF16)
        kpad_ref[PAD_ROWS:, :] = ak_ref[...]
        for blk in range(n_pad_blocks):
            vtpad_ref[blk] = jnp.zeros((ATT_W, Q_BLOCK), BF16)
        for blk in range(avt_ref.shape[1] // Q_BLOCK):
            vtpad_ref[n_pad_blocks + blk] = avt_ref[:, blk * Q_BLOCK:(blk + 1) * Q_BLOCK]
        state_ref[...] = jnp.zeros(state_ref.shape, F32)

    start = pl.multiple_of(g * Q_BLOCK, Q_BLOCK)
    kwin = kpad_ref[pl.ds(start, KEY_WINDOW), :]
    vwin_t = jnp.concatenate([vtpad_ref[g + i] for i in range(KEY_WINDOW // Q_BLOCK)], axis=1)
    pair_w = 2 * ATT_HEAD_DIM
    low_half = lax.broadcasted_iota(jnp.int32, (Q_BLOCK, pair_w), 1) < ATT_HEAD_DIM
    no_prob = jnp.zeros((KEY_WINDOW - HALF_KEYS, HALF_Q), BF16)
    for p in range(ATT_HEADS // 2):
        cols = slice(p * pair_w, (p + 1) * pair_w)
        qp = aq_ref[:, cols]
        kp = kwin[:, cols]
        for e in range(2):
            h = 2 * p + e
            qh = jnp.where(low_half if e == 0 else ~low_half, qp, jnp.zeros_like(qp))
            st = _dot_nt(kp, qh)
            probs, denoms = [], []
            for half in range(2):
                rows = slice(half * HALF_Q, half * HALF_Q + HALF_KEYS)
                lanes = slice(half * HALF_Q, (half + 1) * HALF_Q)
                s = jnp.minimum(st[rows, lanes] + biast_ref[h, rows, lanes], cap_ref[0, rows, lanes])
                prob = jnp.exp(s - jnp.max(s, axis=0, keepdims=True))
                denoms.append(jnp.sum(prob, axis=0, keepdims=True))
                probs.append(prob.astype(BF16))
            prob_t = jnp.concatenate([jnp.concatenate([probs[0], no_prob], axis=0),
                                      jnp.concatenate([no_prob, probs[1]], axis=0)], axis=1)
            head_rows = slice(h * ATT_HEAD_DIM, (h + 1) * ATT_HEAD_DIM)
            out_t = _dot(vwin_t[head_rows, :], prob_t)
            aot_ref[head_rows, :] = (out_t * (1.0 / jnp.concatenate(denoms, axis=1))).astype(BF16)

    for h in range(RET_HEADS):
        kcols = slice(h * RET_KEY_DIM, (h + 1) * RET_KEY_DIM)
        vrows = slice(h * RET_VAL_DIM, (h + 1) * RET_VAL_DIM)
        q = rq_ref[:, kcols]
        k = rk_ref[:, kcols]
        vt = rvt_ref[vrows, :]
        scores_t = _dot_nt(k, q) * dect_ref[h]
        out_t = _dot(vt, scores_t.astype(BF16))
        state_t = state_ref[h]
        qd = (q.astype(F32) * qdec_ref[:, kcols]).astype(BF16)
        out_t = out_t + _dot_nt(state_t.astype(BF16), qd)
        kd = (k.astype(F32) * kdec_ref[:, kcols]).astype(BF16)
        state_ref[h] = state_t * cdec_ref[h] + _dot(vt, kd)
        mu = jnp.mean(out_t, axis=0, keepdims=True)
        dev = out_t - mu
        var = jnp.mean(dev * dev, axis=0, keepdims=True)
        gate = rgt_ref[vrows, :].astype(F32)
        rogt_ref[vrows, :] = ((gate * _sigmoid(gate)) * (dev * lax.rsqrt(var + EPS))).astype(BF16)


def _mixers(rq, rk, rvt, rgt, aq, ak, avt, ebias, cap, dect, qdec, kdec, cdec, batch, seq):
    m = rq.shape[0]
    blocks_per_seq = seq // Q_BLOCK
    n_cap = cap.shape[0]
    qrow = lambda width: pl.BlockSpec((Q_BLOCK, width), lambda b, g: (b * blocks_per_seq + g, 0))
    qcol = lambda height: pl.BlockSpec((height, Q_BLOCK), lambda b, g: (0, b * blocks_per_seq + g))
    cap_spec = pl.BlockSpec((1, KEY_WINDOW, Q_BLOCK), lambda b, g: (jnp.minimum(g, n_cap - 1), 0, 0))
    return pl.pallas_call(
        _mixer_kernel,
        grid=(batch, blocks_per_seq),
        in_specs=[qrow(RET_QK), qrow(RET_QK), qcol(RET_V), qcol(RET_V), qrow(ATT_W),
                  pl.BlockSpec((seq, ATT_W), lambda b, g: (b, 0)),
                  pl.BlockSpec((ATT_W, seq), lambda b, g: (0, b)),
                  _const_spec(ebias.shape), cap_spec, _const_spec(dect.shape),
                  _const_spec(qdec.shape), _const_spec(kdec.shape), _const_spec(cdec.shape)],
        out_specs=[qcol(RET_V), qcol(ATT_W)],
        out_shape=[jax.ShapeDtypeStruct((RET_V, m), BF16), jax.ShapeDtypeStruct((ATT_W, m), BF16)],
        scratch_shapes=[pltpu.VMEM((PAD_ROWS + seq, ATT_W), BF16),
                        pltpu.VMEM(((PAD_ROWS + seq) // Q_BLOCK, ATT_W, Q_BLOCK), BF16),
                        pltpu.VMEM((RET_HEADS, RET_VAL_DIM, RET_KEY_DIM), F32),
                        pltpu.VMEM((ATT_HEADS, KEY_WINDOW, Q_BLOCK), F32)],
        compiler_params=pltpu.CompilerParams(dimension_semantics=("arbitrary", "arbitrary"),
                                             vmem_limit_bytes=VMEM_LIMIT_BYTES),
        name="mixers",
    )(rq, rk, rvt, rgt, aq, ak, avt, ebias, cap, dect, qdec, kdec, cdec)


def _merge_kernel(x_ref, rogt_ref, aot_ref, gl_ref, bg_ref, wro_ref, wao_ref, wout_ref, h_ref):
    y_ret = _dot_tn(rogt_ref[...], wro_ref[...])
    y_att = _dot_tn(aot_ref[...], wao_ref[...])
    gates = _sigmoid(gl_ref[...].astype(F32) + bg_ref[...])
    mixed = gates[:, :D_MODEL] * y_ret + gates[:, D_MODEL:] * y_att
    h_ref[...] = x_ref[...] + _dot(mixed.astype(BF16), wout_ref[...])


def _merge(x2d, rogt, aot, gl, b_gate, w_ret_out, w_att_out, w_out):
    m = x2d.shape[0]
    row = lambda width: pl.BlockSpec((ROW_TILE, width), lambda i: (i, 0))
    col = lambda height: pl.BlockSpec((height, ROW_TILE), lambda i: (0, i))
    return pl.pallas_call(
        _merge_kernel,
        grid=(m // ROW_TILE,),
        in_specs=[row(D_MODEL), col(RET_V), col(ATT_W), row(2 * D_MODEL), _const_spec(b_gate.shape),
                  _const_spec(w_ret_out.shape), _const_spec(w_att_out.shape), _const_spec(w_out.shape)],
        out_specs=row(D_MODEL),
        out_shape=jax.ShapeDtypeStruct((m, D_MODEL), F32),
        compiler_params=pltpu.CompilerParams(dimension_semantics=("arbitrary",),
                                             vmem_limit_bytes=VMEM_LIMIT_BYTES),
        name="merge",
    )(x2d, rogt, aot, gl, b_gate, w_ret_out, w_att_out, w_out)


def _ffn_kernel(h_ref, gf_ref, wg_ref, wu_ref, wd_ref, gn_ref, o_ref):
    h = h_ref[...]
    hn = ((h * _rms_scale(h)) * gf_ref[...]).astype(BF16)
    gate = _dot(hn, wg_ref[...])
    up = _dot(hn, wu_ref[...])
    act = ((gate * _sigmoid(gate)) * up).astype(BF16)
    h2 = h + _dot(act, wd_ref[...])
    o_ref[...] = (h2 * _rms_scale(h2)) * gn_ref[...]


def _ffn(h1, norm_ffn, w_gate, w_up, w_down, norm_final):
    m = h1.shape[0]
    row = pl.BlockSpec((ROW_TILE, D_MODEL), lambda i: (i, 0))
    return pl.pallas_call(
        _ffn_kernel,
        grid=(m // ROW_TILE,),
        in_specs=[row, _const_spec(norm_ffn.shape), _const_spec(w_gate.shape), _const_spec(w_up.shape),
                  _const_spec(w_down.shape), _const_spec(norm_final.shape)],
        out_specs=row,
        out_shape=jax.ShapeDtypeStruct((m, D_MODEL), F32),
        compiler_params=pltpu.CompilerParams(dimension_semantics=("arbitrary",),
                                             vmem_limit_bytes=VMEM_LIMIT_BYTES),
        name="ffn",
    )(h1, norm_ffn, w_gate, w_up, w_down, norm_final)


def _rotary_tables(seq):
    pos = jnp.arange(seq, dtype=F32)
    d = RET_KEY_DIM
    freqs = ROPE_BASE ** (-jnp.arange(0, d, 2, dtype=F32) / d)
    ang = pos[:, None] * freqs[None, :]
    cos = jnp.cos(ang)
    sin = jnp.sin(ang)
    return jnp.concatenate([cos, cos], axis=-1), jnp.concatenate([-sin, sin], axis=-1)


def _retention_tables():
    log_g = jnp.log(1.0 - 2.0 ** (-5.0 - jnp.arange(RET_HEADS, dtype=F32)))
    p = jnp.arange(Q_BLOCK, dtype=F32)
    dist = p[None, :] - p[:, None]
    chunk = np.arange(Q_BLOCK) // CHUNK
    same = jnp.asarray(chunk[:, None] == chunk[None, :])
    earlier = jnp.asarray(chunk[:, None] < chunk[None, :])
    decay = jnp.exp(log_g[:, None, None] * jnp.where(same, jnp.abs(dist), dist)[None])
    dect = jnp.where((same | earlier)[None], decay, 0.0)
    q_dec = jnp.exp(log_g[None, :] * (p[:, None] + 1.0))
    k_dec = jnp.exp(log_g[None, :] * (Q_BLOCK - 1.0 - p[:, None]))
    qdec = jnp.repeat(q_dec, RET_KEY_DIM, axis=1)
    kdec = jnp.repeat(k_dec, RET_KEY_DIM, axis=1)
    cdec = jnp.broadcast_to(jnp.exp(log_g * Q_BLOCK)[:, None, None], (RET_HEADS, 1, RET_KEY_DIM))
    return dect, qdec, kdec, cdec


def _extended_bias(rel_bias):
    left = Q_BLOCK - CHUNK
    return jnp.pad(rel_bias.astype(F32), ((0, 0), (left, BIAS_EXT - left - N_REL)), mode="edge")


def _visibility_caps():
    j = np.arange(KEY_WINDOW)[:, None]
    n = np.arange(Q_BLOCK)[None, :]
    in_band = (j // CHUNK >= n // CHUNK) & (j // CHUNK <= n // CHUNK + BAND_CHUNKS)
    caps = [np.where(in_band & (g * Q_BLOCK - PAD_ROWS + j >= 0), F32_MAX, NEG_INF)
            for g in range(PAD_ROWS // Q_BLOCK + 1)]
    return jnp.asarray(np.stack(caps), F32)


def _split_w_in(w_in):
    sizes = (RET_QK, RET_QK, RET_V, RET_V, ATT_W, ATT_W, ATT_W, 2 * D_MODEL)
    rq, rk, rv, rg, aq, ak, av, gl = jnp.split(w_in, np.cumsum(sizes)[:-1].tolist(), axis=1)
    w_row = jnp.concatenate([rq, rk, aq, ak, gl], axis=1).astype(BF16)
    w_t = jnp.concatenate([rv, rg, av], axis=1).T.astype(BF16)
    return w_row, w_t


def kernel(x, norm_mix, w_in, b_gate, rel_bias, w_ret_out, w_att_out, w_out, norm_ffn,
           w_ffn_gate, w_ffn_up, w_ffn_down, norm_final):
    batch, seq, d_model = x.shape
    assert w_in.shape[0] == 1, "single-layer block"
    assert d_model == D_MODEL and seq % ROW_TILE == 0 and seq % Q_BLOCK == 0 and PAD_ROWS % Q_BLOCK == 0
    cos, sin = _rotary_tables(seq)
    dect, qdec, kdec, cdec = _retention_tables()
    w_row, w_t = _split_w_in(w_in[0])
    x2d = x.reshape(batch * seq, d_model)
    rq, rk, aq, ak, gl, rvt, rgt, avt = _in_proj(x2d, norm_mix, w_row, w_t, cos, sin, seq)
    rogt, aot = _mixers(rq, rk, rvt, rgt, aq, ak, avt, _extended_bias(rel_bias[0]), _visibility_caps(),
                        dect, qdec, kdec, cdec, batch, seq)
    h1 = _merge(x2d, rogt, aot, gl, b_gate, w_ret_out[0].astype(BF16),
                w_att_out[0].astype(BF16), w_out[0].astype(BF16))
    out = _ffn(h1, norm_ffn, w_ffn_gate[0].astype(BF16), w_ffn_up[0].astype(BF16),
               w_ffn_down[0].astype(BF16), norm_final[None, :])
    return out.reshape(batch, seq, d_model)
```

```python
import jax
import jax.numpy as jnp
import numpy as np
from jax import lax
from jax.experimental import pallas as pl
from jax.experimental.pallas import tpu as pltpu

D_MODEL = 1024
CHUNK = 64
RET_HEADS = 4
RET_KEY_DIM = 128
RET_VAL_DIM = 256
RET_QK = RET_HEADS * RET_KEY_DIM
RET_V = RET_HEADS * RET_VAL_DIM
ATT_HEADS = 8
ATT_HEAD_DIM = 64
ATT_W = ATT_HEADS * ATT_HEAD_DIM
BAND_CHUNKS = 8
MAX_REL = 256
N_REL = CHUNK + MAX_REL
ROPE_BASE = 10000.0
EPS = 1e-6
NEG_INF = -1e30
F32_MAX = float(np.finfo(np.float32).max)

V7X_LANES = 128
VMEM_LIMIT_BYTES = 56 * 1024 * 1024

ROW_TILE = 512
COL_CHUNK = 512
Q_BLOCK = 4 * CHUNK
PAD_ROWS = BAND_CHUNKS * CHUNK
KEY_WINDOW = Q_BLOCK + PAD_ROWS
HALF_Q = Q_BLOCK // 2
HALF_KEYS = HALF_Q + PAD_ROWS
BIAS_EXT = 1024
BIAS_ROLL_ROWS = 128
DENOM_ROWS = 16
LOG2E = 1.4426950408889634

assert HALF_Q == V7X_LANES and KEY_WINDOW - HALF_KEYS == HALF_Q
assert BIAS_EXT >= KEY_WINDOW + Q_BLOCK - 1

BF16 = jnp.bfloat16
F32 = jnp.float32


def _const_spec(shape):
    zeros = (0,) * len(shape)
    return pl.BlockSpec(shape, lambda *_: zeros, pipeline_mode=pl.Buffered(1))


def _dot(a, b):
    return jnp.dot(a, b, preferred_element_type=F32)


def _dot_nt(a, b):
    return lax.dot_general(a, b, (((1,), (1,)), ((), ())), preferred_element_type=F32)


def _dot_tn(a, b):
    return lax.dot_general(a, b, (((0,), (0,)), ((), ())), preferred_element_type=F32)


def _sigmoid(x):
    return 1.0 / (1.0 + jnp.exp(-x))


def _rms_scale(x):
    return lax.rsqrt(jnp.mean(x * x, axis=-1, keepdims=True) + EPS)


def _in_proj_kernel(x_ref, g_ref, w_row_ref, w_t_ref, cos_ref, sin_ref,
                    rq_ref, rk_ref, aq_ref, ak_ref, gl_ref, rvt_ref, rgt_ref, avt_ref):
    x = x_ref[...]
    xn = ((x * _rms_scale(x)) * g_ref[...]).astype(BF16)
    cos = cos_ref[...]
    sin = sin_ref[...]

    def rotary_store(out_ref, col0, scale):
        t = _dot(xn, w_row_ref[:, col0:col0 + RET_QK])
        for h in range(RET_HEADS):
            cols = slice(h * RET_KEY_DIM, (h + 1) * RET_KEY_DIM)
            th = t[:, cols]
            rot = th * cos + pltpu.roll(th, RET_KEY_DIM // 2, axis=1) * sin
            out_ref[:, cols] = (rot if scale is None else rot * scale).astype(BF16)

    rotary_store(rq_ref, 0, None)
    rotary_store(rk_ref, RET_QK, RET_KEY_DIM ** -0.5)
    col = 2 * RET_QK
    for out_ref, width, scale in ((aq_ref, ATT_W, ATT_HEAD_DIM ** -0.5 * LOG2E), (ak_ref, ATT_W, None),
                                  (gl_ref, 2 * D_MODEL, None)):
        for c in range(0, width, COL_CHUNK):
            t = _dot(xn, w_row_ref[:, col + c:col + c + COL_CHUNK])
            out_ref[:, c:c + COL_CHUNK] = (t if scale is None else t * scale).astype(BF16)
        col += width
    row = 0
    for out_ref, height in ((rvt_ref, RET_V), (rgt_ref, RET_V), (avt_ref, ATT_W)):
        for r in range(0, height, COL_CHUNK):
            t = _dot_nt(w_t_ref[row + r:row + r + COL_CHUNK, :], xn)
            out_ref[r:r + COL_CHUNK, :] = t.astype(BF16)
        row += height


def _in_proj(x2d, g, w_row, w_t, cos, sin, seq):
    m = x2d.shape[0]
    steps_per_seq = seq // ROW_TILE
    row = lambda width: pl.BlockSpec((ROW_TILE, width), lambda i: (i, 0))
    col = lambda height: pl.BlockSpec((height, ROW_TILE), lambda i: (0, i))
    pos = pl.BlockSpec((ROW_TILE, RET_KEY_DIM), lambda i: (i % steps_per_seq, 0))
    row_widths = (RET_QK, RET_QK, ATT_W, ATT_W, 2 * D_MODEL)
    col_heights = (RET_V, RET_V, ATT_W)
    return pl.pallas_call(
        _in_proj_kernel,
        grid=(m // ROW_TILE,),
        in_specs=[row(D_MODEL), _const_spec(g.shape), _const_spec(w_row.shape), _const_spec(w_t.shape),
                  pos, pos],
        out_specs=[row(w) for w in row_widths] + [col(h) for h in col_heights],
        out_shape=([jax.ShapeDtypeStruct((m, w), BF16) for w in row_widths]
                   + [jax.ShapeDtypeStruct((h, m), BF16) for h in col_heights]),
        compiler_params=pltpu.CompilerParams(dimension_semantics=("arbitrary",),
                                             vmem_limit_bytes=VMEM_LIMIT_BYTES),
        name="in_proj",
    )(x2d, g, w_row, w_t, cos, sin)


def _mixer_kernel(rq_ref, rk_ref, rvt_ref, rgt_ref, aq_ref, ak_ref, avt_ref,
                  ebias_ref, cap_ref, dect_ref, qdec_ref, kdec_ref, cdec_ref,
                  rogt_ref, aot_ref, kpad_ref, vtpad_ref, state_ref, biast_ref):
    b = pl.program_id(0)
    g = pl.program_id(1)
    n_pad_blocks = PAD_ROWS // Q_BLOCK

    @pl.when((b == 0) & (g == 0))
    def _():
        for h in range(ATT_HEADS):
            ext = jnp.broadcast_to(ebias_ref[h:h + 1, :], (BIAS_ROLL_ROWS, BIAS_EXT))
            for r in range(0, KEY_WINDOW, BIAS_ROLL_ROWS):
                rolled = pltpu.roll(ext, BIAS_EXT - KEY_WINDOW + 1 + r, axis=1, stride=1, stride_axis=0)
                biast_ref[h, r:r + BIAS_ROLL_ROWS, :] = rolled[:, :Q_BLOCK]

    @pl.when(g == 0)
    def _():
        kpad_ref[0:PAD_ROWS, :] = jnp.zeros((PAD_ROWS, ATT_W), BF16)
        kpad_ref[PAD_ROWS:, :] = ak_ref[...]
        for blk in range(n_pad_blocks):
            vtpad_ref[blk] = jnp.zeros((ATT_W, Q_BLOCK), BF16)
        for blk in range(avt_ref.shape[1] // Q_BLOCK):
            vtpad_ref[n_pad_blocks + blk] = avt_ref[:, blk * Q_BLOCK:(blk + 1) * Q_BLOCK]
        state_ref[...] = jnp.zeros(state_ref.shape, F32)

    start = pl.multiple_of(g * Q_BLOCK, Q_BLOCK)
    kwin = kpad_ref[pl.ds(start, KEY_WINDOW), :]
    vwin_t = jnp.concatenate([vtpad_ref[g + i] for i in range(KEY_WINDOW // Q_BLOCK)], axis=1)
    pair_w = 2 * ATT_HEAD_DIM
    low_half = lax.broadcasted_iota(jnp.int32, (Q_BLOCK, pair_w), 1) < ATT_HEAD_DIM
    no_prob = jnp.zeros((KEY_WINDOW - HALF_KEYS, HALF_Q), BF16)
    ones_rows = jnp.ones((DENOM_ROWS, KEY_WINDOW), BF16)

    def attention_scores(p):
        cols = slice(p * pair_w, (p + 1) * pair_w)
        qp = aq_ref[:, cols]
        zero = jnp.zeros_like(qp)
        q2 = jnp.concatenate([jnp.where(low_half, qp, zero), jnp.where(low_half, zero, qp)], axis=0)
        return _dot_nt(kwin[:, cols], q2)

    def attention_finish(p, st_pair):
        for e in range(2):
            h = 2 * p + e
            probs = []
            for half in range(2):
                rows = slice(half * HALF_Q, half * HALF_Q + HALF_KEYS)
                lanes = slice(half * HALF_Q, (half + 1) * HALF_Q)
                st = st_pair[rows, e * Q_BLOCK + half * HALF_Q:e * Q_BLOCK + (half + 1) * HALF_Q]
                s = jnp.minimum(st + biast_ref[h, rows, lanes], cap_ref[0, rows, lanes])
                probs.append(jnp.exp2(s - jnp.max(s, axis=0, keepdims=True)).astype(BF16))
            prob_t = jnp.concatenate([jnp.concatenate([probs[0], no_prob], axis=0),
                                      jnp.concatenate([no_prob, probs[1]], axis=0)], axis=1)
            head_rows = slice(h * ATT_HEAD_DIM, (h + 1) * ATT_HEAD_DIM)
            out_t = _dot(jnp.concatenate([vwin_t[head_rows, :], ones_rows], axis=0), prob_t)
            denom = out_t[ATT_HEAD_DIM:ATT_HEAD_DIM + 1, :]
            aot_ref[head_rows, :] = (out_t[:ATT_HEAD_DIM, :] * (1.0 / denom)).astype(BF16)

    def retention_dots(h):
        kcols = slice(h * RET_KEY_DIM, (h + 1) * RET_KEY_DIM)
        q = rq_ref[:, kcols]
        k = rk_ref[:, kcols]
        vt = rvt_ref[h * RET_VAL_DIM:(h + 1) * RET_VAL_DIM, :]
        state_t = state_ref[h]
        qd = (q.astype(F32) * qdec_ref[:, kcols]).astype(BF16)
        kd = (k.astype(F32) * kdec_ref[:, kcols]).astype(BF16)
        scores_t = _dot_nt(k, q)
        carried = _dot_nt(state_t.astype(BF16), qd)
        state_ref[h] = state_t * cdec_ref[h] + _dot(vt, kd)
        return scores_t, carried

    def retention_finish(h, scores_t, carried):
        vrows = slice(h * RET_VAL_DIM, (h + 1) * RET_VAL_DIM)
        out_t = _dot(rvt_ref[vrows, :], (scores_t * dect_ref[h]).astype(BF16)) + carried
        mu = jnp.mean(out_t, axis=0, keepdims=True)
        dev = out_t - mu
        var = jnp.mean(dev * dev, axis=0, keepdims=True)
        gate = rgt_ref[vrows, :].astype(F32)
        rogt_ref[vrows, :] = ((gate * _sigmoid(gate)) * (dev * lax.rsqrt(var + EPS))).astype(BF16)

    assert ATT_HEADS // 2 == RET_HEADS
    att_next = attention_scores(0)
    ret_next = retention_dots(0)
    for i in range(RET_HEADS):
        att_now, ret_now = att_next, ret_next
        if i + 1 < RET_HEADS:
            att_next = attention_scores(i + 1)
            ret_next = retention_dots(i + 1)
        attention_finish(i, att_now)
        retention_finish(i, *ret_now)


def _mixers(rq, rk, rvt, rgt, aq, ak, avt, ebias, cap, dect, qdec, kdec, cdec, batch, seq):
    m = rq.shape[0]
    blocks_per_seq = seq // Q_BLOCK
    n_cap = cap.shape[0]
    qrow = lambda width: pl.BlockSpec((Q_BLOCK, width), lambda b, g: (b * blocks_per_seq + g, 0))
    qcol = lambda height: pl.BlockSpec((height, Q_BLOCK), lambda b, g: (0, b * blocks_per_seq + g))
    cap_spec = pl.BlockSpec((1, KEY_WINDOW, Q_BLOCK), lambda b, g: (jnp.minimum(g, n_cap - 1), 0, 0))
    return pl.pallas_call(
        _mixer_kernel,
        grid=(batch, blocks_per_seq),
        in_specs=[qrow(RET_QK), qrow(RET_QK), qcol(RET_V), qcol(RET_V), qrow(ATT_W),
                  pl.BlockSpec((seq, ATT_W), lambda b, g: (b, 0)),
                  pl.BlockSpec((ATT_W, seq), lambda b, g: (0, b)),
                  _const_spec(ebias.shape), cap_spec, _const_spec(dect.shape),
                  _const_spec(qdec.shape), _const_spec(kdec.shape), _const_spec(cdec.shape)],
        out_specs=[qcol(RET_V), qcol(ATT_W)],
        out_shape=[jax.ShapeDtypeStruct((RET_V, m), BF16), jax.ShapeDtypeStruct((ATT_W, m), BF16)],
        scratch_shapes=[pltpu.VMEM((PAD_ROWS + seq, ATT_W), BF16),
                        pltpu.VMEM(((PAD_ROWS + seq) // Q_BLOCK, ATT_W, Q_BLOCK), BF16),
                        pltpu.VMEM((RET_HEADS, RET_VAL_DIM, RET_KEY_DIM), F32),
                        pltpu.VMEM((ATT_HEADS, KEY_WINDOW, Q_BLOCK), F32)],
        compiler_params=pltpu.CompilerParams(dimension_semantics=("arbitrary", "arbitrary"),
                                             vmem_limit_bytes=VMEM_LIMIT_BYTES),
        name="mixers",
    )(rq, rk, rvt, rgt, aq, ak, avt, ebias, cap, dect, qdec, kdec, cdec)


def _merge_kernel(x_ref, rogt_ref, aot_ref, gl_ref, bg_ref, wro_ref, wao_ref, wout_ref, h_ref):
    y_ret = _dot_tn(rogt_ref[...], wro_ref[...])
    y_att = _dot_tn(aot_ref[...], wao_ref[...])
    gates = _sigmoid(gl_ref[...].astype(F32) + bg_ref[...])
    mixed = gates[:, :D_MODEL] * y_ret + gates[:, D_MODEL:] * y_att
    h_ref[...] = x_ref[...] + _dot(mixed.astype(BF16), wout_ref[...])


def _merge(x2d, rogt, aot, gl, b_gate, w_ret_out, w_att_out, w_out):
    m = x2d.shape[0]
    row = lambda width: pl.BlockSpec((ROW_TILE, width), lambda i: (i, 0))
    col = lambda height: pl.BlockSpec((height, ROW_TILE), lambda i: (0, i))
    return pl.pallas_call(
        _merge_kernel,
        grid=(m // ROW_TILE,),
        in_specs=[row(D_MODEL), col(RET_V), col(ATT_W), row(2 * D_MODEL), _const_spec(b_gate.shape),
                  _const_spec(w_ret_out.shape), _const_spec(w_att_out.shape), _const_spec(w_out.shape)],
        out_specs=row(D_MODEL),
        out_shape=jax.ShapeDtypeStruct((m, D_MODEL), F32),
        compiler_params=pltpu.CompilerParams(dimension_semantics=("arbitrary",),
                                             vmem_limit_bytes=VMEM_LIMIT_BYTES),
        name="merge",
    )(x2d, rogt, aot, gl, b_gate, w_ret_out, w_att_out, w_out)


def _ffn_kernel(h_ref, gf_ref, wg_ref, wu_ref, wd_ref, gn_ref, o_ref):
    h = h_ref[...]
    hn = ((h * _rms_scale(h)) * gf_ref[...]).astype(BF16)
    gate = _dot(hn, wg_ref[...])
    up = _dot(hn, wu_ref[...])
    act = ((gate * _sigmoid(gate)) * up).astype(BF16)
    h2 = h + _dot(act, wd_ref[...])
    o_ref[...] = (h2 * _rms_scale(h2)) * gn_ref[...]


def _ffn(h1, norm_ffn, w_gate, w_up, w_down, norm_final):
    m = h1.shape[0]
    row = pl.BlockSpec((ROW_TILE, D_MODEL), lambda i: (i, 0))
    return pl.pallas_call(
        _ffn_kernel,
        grid=(m // ROW_TILE,),
        in_specs=[row, _const_spec(norm_ffn.shape), _const_spec(w_gate.shape), _const_spec(w_up.shape),
                  _const_spec(w_down.shape), _const_spec(norm_final.shape)],
        out_specs=row,
        out_shape=jax.ShapeDtypeStruct((m, D_MODEL), F32),
        compiler_params=pltpu.CompilerParams(dimension_semantics=("arbitrary",),
                                             vmem_limit_bytes=VMEM_LIMIT_BYTES),
        name="ffn",
    )(h1, norm_ffn, w_gate, w_up, w_down, norm_final)


def _rotary_tables(seq):
    pos = jnp.arange(seq, dtype=F32)
    d = RET_KEY_DIM
    freqs = ROPE_BASE ** (-jnp.arange(0, d, 2, dtype=F32) / d)
    ang = pos[:, None] * freqs[None, :]
    cos = jnp.cos(ang)
    sin = jnp.sin(ang)
    return jnp.concatenate([cos, cos], axis=-1), jnp.concatenate([-sin, sin], axis=-1)


def _retention_tables():
    log_g = jnp.log(1.0 - 2.0 ** (-5.0 - jnp.arange(RET_HEADS, dtype=F32)))
    p = jnp.arange(Q_BLOCK, dtype=F32)
    dist = p[None, :] - p[:, None]
    chunk = np.arange(Q_BLOCK) // CHUNK
    same = jnp.asarray(chunk[:, None] == chunk[None, :])
    earlier = jnp.asarray(chunk[:, None] < chunk[None, :])
    decay = jnp.exp(log_g[:, None, None] * jnp.where(same, jnp.abs(dist), dist)[None])
    dect = jnp.where((same | earlier)[None], decay, 0.0)
    q_dec = jnp.exp(log_g[None, :] * (p[:, None] + 1.0))
    k_dec = jnp.exp(log_g[None, :] * (Q_BLOCK - 1.0 - p[:, None]))
    qdec = jnp.repeat(q_dec, RET_KEY_DIM, axis=1)
    kdec = jnp.repeat(k_dec, RET_KEY_DIM, axis=1)
    cdec = jnp.broadcast_to(jnp.exp(log_g * Q_BLOCK)[:, None, None], (RET_HEADS, 1, RET_KEY_DIM))
    return dect, qdec, kdec, cdec


def _extended_bias(rel_bias):
    left = Q_BLOCK - CHUNK
    return jnp.pad(rel_bias.astype(F32) * LOG2E,((0, 0), (left, BIAS_EXT - left - N_REL)), mode="edge")


def _visibility_caps():
    j = np.arange(KEY_WINDOW)[:, None]
    n = np.arange(Q_BLOCK)[None, :]
    in_band = (j // CHUNK >= n // CHUNK) & (j // CHUNK <= n // CHUNK + BAND_CHUNKS)
    caps = [np.where(in_band & (g * Q_BLOCK - PAD_ROWS + j >= 0), F32_MAX, NEG_INF)
            for g in range(PAD_ROWS // Q_BLOCK + 1)]
    return jnp.asarray(np.stack(caps), F32)


def _split_w_in(w_in):
    sizes = (RET_QK, RET_QK, RET_V, RET_V, ATT_W, ATT_W, ATT_W, 2 * D_MODEL)
    rq, rk, rv, rg, aq, ak, av, gl = jnp.split(w_in, np.cumsum(sizes)[:-1].tolist(), axis=1)
    w_row = jnp.concatenate([rq, rk, aq, ak, gl], axis=1).astype(BF16)
    w_t = jnp.concatenate([rv, rg, av], axis=1).T.astype(BF16)
    return w_row, w_t


def kernel(x, norm_mix, w_in, b_gate, rel_bias, w_ret_out, w_att_out, w_out, norm_ffn,
           w_ffn_gate, w_ffn_up, w_ffn_down, norm_final):
    batch, seq, d_model = x.shape
    assert w_in.shape[0] == 1, "single-layer block"
    assert d_model == D_MODEL and seq % ROW_TILE == 0 and seq % Q_BLOCK == 0 and PAD_ROWS % Q_BLOCK == 0
    cos, sin = _rotary_tables(seq)
    dect, qdec, kdec, cdec = _retention_tables()
    w_row, w_t = _split_w_in(w_in[0])
    x2d = x.reshape(batch * seq, d_model)
    rq, rk, aq, ak, gl, rvt, rgt, avt = _in_proj(x2d, norm_mix, w_row, w_t, cos, sin, seq)
    rogt, aot = _mixers(rq, rk, rvt, rgt, aq, ak, avt, _extended_bias(rel_bias[0]), _visibility_caps(),
                        dect, qdec, kdec, cdec, batch, seq)
    h1 = _merge(x2d, rogt, aot, gl, b_gate, w_ret_out[0].astype(BF16),
                w_att_out[0].astype(BF16), w_out[0].astype(BF16))
    out = _ffn(h1, norm_ffn, w_ffn_gate[0].astype(BF16), w_ffn_up[0].astype(BF16),
               w_ffn_down[0].astype(BF16), norm_final[None, :])
    return out.reshape(batch, seq, d_model)
```

```python
import jax
import jax.numpy as jnp
import numpy as np
from jax import lax
from jax.experimental import pallas as pl
from jax.experimental.pallas import tpu as pltpu

D_MODEL = 1024
CHUNK = 64
RET_HEADS = 4
RET_KEY_DIM = 128
RET_VAL_DIM = 256
RET_QK = RET_HEADS * RET_KEY_DIM
RET_V = RET_HEADS * RET_VAL_DIM
ATT_HEADS = 8
ATT_HEAD_DIM = 64
ATT_W = ATT_HEADS * ATT_HEAD_DIM
BAND_CHUNKS = 8
MAX_REL = 256
N_REL = CHUNK + MAX_REL
ROPE_BASE = 10000.0
EPS = 1e-6
NEG_INF = -1e30
F32_MAX = float(np.finfo(np.float32).max)

V7X_LANES = 128
VMEM_LIMIT_BYTES = 56 * 1024 * 1024

ROW_TILE = 512
COL_CHUNK = 512
V7X_MXU_DIM = 256
FFN_CHUNK = 6 * V7X_MXU_DIM
Q_BLOCK = 4 * CHUNK
PAD_ROWS = BAND_CHUNKS * CHUNK
KEY_WINDOW = Q_BLOCK + PAD_ROWS
HALF_Q = Q_BLOCK // 2
HALF_KEYS = HALF_Q + PAD_ROWS
BIAS_EXT = 1024
BIAS_ROLL_ROWS = 128
DENOM_ROWS = 16
LOG2E = 1.4426950408889634

assert HALF_Q == V7X_LANES and KEY_WINDOW - HALF_KEYS == HALF_Q
assert BIAS_EXT >= KEY_WINDOW + Q_BLOCK - 1

BF16 = jnp.bfloat16
F32 = jnp.float32


def _const_spec(shape):
    zeros = (0,) * len(shape)
    return pl.BlockSpec(shape, lambda *_: zeros, pipeline_mode=pl.Buffered(1))


def _dot(a, b):
    return jnp.dot(a, b, preferred_element_type=F32)


def _dot_nt(a, b):
    return lax.dot_general(a, b, (((1,), (1,)), ((), ())), preferred_element_type=F32)


def _dot_tn(a, b):
    return lax.dot_general(a, b, (((0,), (0,)), ((), ())), preferred_element_type=F32)


def _sigmoid(x):
    return 1.0 / (1.0 + jnp.exp(-x))


def _rms_scale(x):
    return lax.rsqrt(jnp.mean(x * x, axis=-1, keepdims=True) + EPS)


def _in_proj_kernel(x_ref, g_ref, w_row_ref, w_t_ref, cos_ref, sin_ref,
                    rq_ref, rk_ref, aq_ref, ak_ref, gl_ref, rvt_ref, rgt_ref, avt_ref):
    x = x_ref[...]
    xn = ((x * _rms_scale(x)) * g_ref[...]).astype(BF16)
    cos = cos_ref[...]
    sin = sin_ref[...]

    def rotary_store(out_ref, col0, scale):
        t = _dot(xn, w_row_ref[:, col0:col0 + RET_QK])
        for h in range(RET_HEADS):
            cols = slice(h * RET_KEY_DIM, (h + 1) * RET_KEY_DIM)
            th = t[:, cols]
            rot = th * cos + pltpu.roll(th, RET_KEY_DIM // 2, axis=1) * sin
            out_ref[:, cols] = (rot if scale is None else rot * scale).astype(BF16)

    rotary_store(rq_ref, 0, None)
    rotary_store(rk_ref, RET_QK, RET_KEY_DIM ** -0.5)
    col = 2 * RET_QK
    for out_ref, width, scale in ((aq_ref, ATT_W, ATT_HEAD_DIM ** -0.5 * LOG2E), (ak_ref, ATT_W, None),
                                  (gl_ref, 2 * D_MODEL, None)):
        for c in range(0, width, COL_CHUNK):
            t = _dot(xn, w_row_ref[:, col + c:col + c + COL_CHUNK])
            out_ref[:, c:c + COL_CHUNK] = (t if scale is None else t * scale).astype(BF16)
        col += width
    row = 0
    for out_ref, height in ((rvt_ref, RET_V), (rgt_ref, RET_V), (avt_ref, ATT_W)):
        for r in range(0, height, COL_CHUNK):
            t = _dot_nt(w_t_ref[row + r:row + r + COL_CHUNK, :], xn)
            out_ref[r:r + COL_CHUNK, :] = t.astype(BF16)
        row += height


def _in_proj(x2d, g, w_row, w_t, cos, sin, seq):
    m = x2d.shape[0]
    steps_per_seq = seq // ROW_TILE
    row = lambda width: pl.BlockSpec((ROW_TILE, width), lambda i: (i, 0))
    col = lambda height: pl.BlockSpec((height, ROW_TILE), lambda i: (0, i))
    pos = pl.BlockSpec((ROW_TILE, RET_KEY_DIM), lambda i: (i % steps_per_seq, 0))
    row_widths = (RET_QK, RET_QK, ATT_W, ATT_W, 2 * D_MODEL)
    col_heights = (RET_V, RET_V, ATT_W)
    return pl.pallas_call(
        _in_proj_kernel,
        grid=(m // ROW_TILE,),
        in_specs=[row(D_MODEL), _const_spec(g.shape), _const_spec(w_row.shape), _const_spec(w_t.shape),
                  pos, pos],
        out_specs=[row(w) for w in row_widths] + [col(h) for h in col_heights],
        out_shape=([jax.ShapeDtypeStruct((m, w), BF16) for w in row_widths]
                   + [jax.ShapeDtypeStruct((h, m), BF16) for h in col_heights]),
        compiler_params=pltpu.CompilerParams(dimension_semantics=("arbitrary",),
                                             vmem_limit_bytes=VMEM_LIMIT_BYTES),
        name="in_proj",
    )(x2d, g, w_row, w_t, cos, sin)


def _mixer_kernel(rq_ref, rk_ref, rvt_ref, rgt_ref, aq_ref, ak_ref, avt_ref,
                  ebias_ref, cap_ref, dect_ref, qdec_ref, kdec_ref, cdec_ref,
                  rogt_ref, aot_ref, kpad_ref, vtpad_ref, state_ref, biast_ref):
    b = pl.program_id(0)
    g = pl.program_id(1)
    n_pad_blocks = PAD_ROWS // Q_BLOCK

    @pl.when((b == 0) & (g == 0))
    def _():
        for h in range(ATT_HEADS):
            ext = jnp.broadcast_to(ebias_ref[h:h + 1, :], (BIAS_ROLL_ROWS, BIAS_EXT))
            for r in range(0, KEY_WINDOW, BIAS_ROLL_ROWS):
                rolled = pltpu.roll(ext, BIAS_EXT - KEY_WINDOW + 1 + r, axis=1, stride=1, stride_axis=0)
                biast_ref[h, r:r + BIAS_ROLL_ROWS, :] = rolled[:, :Q_BLOCK]

    @pl.when(g == 0)
    def _():
        kpad_ref[0:PAD_ROWS, :] = jnp.zeros((PAD_ROWS, ATT_W), BF16)
        kpad_ref[PAD_ROWS:, :] = ak_ref[...]
        for blk in range(n_pad_blocks):
            vtpad_ref[blk] = jnp.zeros((ATT_W, Q_BLOCK), BF16)
        for blk in range(avt_ref.shape[1] // Q_BLOCK):
            vtpad_ref[n_pad_blocks + blk] = avt_ref[:, blk * Q_BLOCK:(blk + 1) * Q_BLOCK]
        state_ref[...] = jnp.zeros(state_ref.shape, F32)

    start = pl.multiple_of(g * Q_BLOCK, Q_BLOCK)
    kwin = kpad_ref[pl.ds(start, KEY_WINDOW), :]
    vwin_t = jnp.concatenate([vtpad_ref[g + i] for i in range(KEY_WINDOW // Q_BLOCK)], axis=1)
    pair_w = 2 * ATT_HEAD_DIM
    low_half = lax.broadcasted_iota(jnp.int32, (Q_BLOCK, pair_w), 1) < ATT_HEAD_DIM
    no_prob = jnp.zeros((KEY_WINDOW - HALF_KEYS, HALF_Q), BF16)
    ones_rows = jnp.ones((DENOM_ROWS, KEY_WINDOW), BF16)

    def attention_scores(p):
        cols = slice(p * pair_w, (p + 1) * pair_w)
        qp = aq_ref[:, cols]
        zero = jnp.zeros_like(qp)
        q2 = jnp.concatenate([jnp.where(low_half, qp, zero), jnp.where(low_half, zero, qp)], axis=0)
        return _dot_nt(kwin[:, cols], q2)

    def attention_finish(p, st_pair):
        for e in range(2):
            h = 2 * p + e
            probs = []
            for half in range(2):
                rows = slice(half * HALF_Q, half * HALF_Q + HALF_KEYS)
                lanes = slice(half * HALF_Q, (half + 1) * HALF_Q)
                st = st_pair[rows, e * Q_BLOCK + half * HALF_Q:e * Q_BLOCK + (half + 1) * HALF_Q]
                s = jnp.minimum(st + biast_ref[h, rows, lanes], cap_ref[0, rows, lanes])
                probs.append(jnp.exp2(s - jnp.max(s, axis=0, keepdims=True)).astype(BF16))
            prob_t = jnp.concatenate([jnp.concatenate([probs[0], no_prob], axis=0),
                                      jnp.concatenate([no_prob, probs[1]], axis=0)], axis=1)
            head_rows = slice(h * ATT_HEAD_DIM, (h + 1) * ATT_HEAD_DIM)
            out_t = _dot(jnp.concatenate([vwin_t[head_rows, :], ones_rows], axis=0), prob_t)
            denom = out_t[ATT_HEAD_DIM:ATT_HEAD_DIM + 1, :]
            aot_ref[head_rows, :] = (out_t[:ATT_HEAD_DIM, :] * (1.0 / denom)).astype(BF16)

    def retention_dots(h):
        kcols = slice(h * RET_KEY_DIM, (h + 1) * RET_KEY_DIM)
        q = rq_ref[:, kcols]
        k = rk_ref[:, kcols]
        vt = rvt_ref[h * RET_VAL_DIM:(h + 1) * RET_VAL_DIM, :]
        state_t = state_ref[h]
        qd = (q.astype(F32) * qdec_ref[:, kcols]).astype(BF16)
        kd = (k.astype(F32) * kdec_ref[:, kcols]).astype(BF16)
        scores_t = _dot_nt(k, q)
        carried = _dot_nt(state_t.astype(BF16), qd)
        state_ref[h] = state_t * cdec_ref[h] + _dot(vt, kd)
        return scores_t, carried

    def retention_finish(h, scores_t, carried):
        vrows = slice(h * RET_VAL_DIM, (h + 1) * RET_VAL_DIM)
        out_t = _dot(rvt_ref[vrows, :], (scores_t * dect_ref[h]).astype(BF16)) + carried
        mu = jnp.mean(out_t, axis=0, keepdims=True)
        dev = out_t - mu
        var = jnp.mean(dev * dev, axis=0, keepdims=True)
        gate = rgt_ref[vrows, :].astype(F32)
        rogt_ref[vrows, :] = ((gate * _sigmoid(gate)) * (dev * lax.rsqrt(var + EPS))).astype(BF16)

    assert ATT_HEADS // 2 == RET_HEADS
    att_next = attention_scores(0)
    ret_next = retention_dots(0)
    for i in range(RET_HEADS):
        att_now, ret_now = att_next, ret_next
        if i + 1 < RET_HEADS:
            att_next = attention_scores(i + 1)
            ret_next = retention_dots(i + 1)
        attention_finish(i, att_now)
        retention_finish(i, *ret_now)


def _mixers(rq, rk, rvt, rgt, aq, ak, avt, ebias, cap, dect, qdec, kdec, cdec, batch, seq):
    m = rq.shape[0]
    blocks_per_seq = seq // Q_BLOCK
    n_cap = cap.shape[0]
    qrow = lambda width: pl.BlockSpec((Q_BLOCK, width), lambda b, g: (b * blocks_per_seq + g, 0))
    qcol = lambda height: pl.BlockSpec((height, Q_BLOCK), lambda b, g: (0, b * blocks_per_seq + g))
    cap_spec = pl.BlockSpec((1, KEY_WINDOW, Q_BLOCK), lambda b, g: (jnp.minimum(g, n_cap - 1), 0, 0))
    return pl.pallas_call(
        _mixer_kernel,
        grid=(batch, blocks_per_seq),
        in_specs=[qrow(RET_QK), qrow(RET_QK), qcol(RET_V), qcol(RET_V), qrow(ATT_W),
                  pl.BlockSpec((seq, ATT_W), lambda b, g: (b, 0)),
                  pl.BlockSpec((ATT_W, seq), lambda b, g: (0, b)),
                  _const_spec(ebias.shape), cap_spec, _const_spec(dect.shape),
                  _const_spec(qdec.shape), _const_spec(kdec.shape), _const_spec(cdec.shape)],
        out_specs=[qcol(RET_V), qcol(ATT_W)],
        out_shape=[jax.ShapeDtypeStruct((RET_V, m), BF16), jax.ShapeDtypeStruct((ATT_W, m), BF16)],
        scratch_shapes=[pltpu.VMEM((PAD_ROWS + seq, ATT_W), BF16),
                        pltpu.VMEM(((PAD_ROWS + seq) // Q_BLOCK, ATT_W, Q_BLOCK), BF16),
                        pltpu.VMEM((RET_HEADS, RET_VAL_DIM, RET_KEY_DIM), F32),
                        pltpu.VMEM((ATT_HEADS, KEY_WINDOW, Q_BLOCK), F32)],
        compiler_params=pltpu.CompilerParams(dimension_semantics=("arbitrary", "arbitrary"),
                                             vmem_limit_bytes=VMEM_LIMIT_BYTES),
        name="mixers",
    )(rq, rk, rvt, rgt, aq, ak, avt, ebias, cap, dect, qdec, kdec, cdec)


def _merge_ffn_kernel(x_ref, rogt_ref, aot_ref, gl_ref, bg_ref, wro_ref, wao_ref, wout_ref,
                      gf_ref, wg_ref, wu_ref, wd_ref, gn_ref, o_ref):
    y_ret = _dot_tn(rogt_ref[...], wro_ref[...])
    y_att = _dot_tn(aot_ref[...], wao_ref[...])
    g_ret = _sigmoid(gl_ref[:, :D_MODEL].astype(F32) + bg_ref[:, :D_MODEL])
    g_att = _sigmoid(gl_ref[:, D_MODEL:].astype(F32) + bg_ref[:, D_MODEL:])
    mixed = (g_ret * y_ret + g_att * y_att).astype(BF16)
    h = x_ref[...] + _dot(mixed, wout_ref[...])
    hn = ((h * _rms_scale(h)) * gf_ref[...]).astype(BF16)
    d_ff = wg_ref.shape[1]
    ffn = None
    for c in range(0, d_ff, FFN_CHUNK):
        cols = slice(c, min(c + FFN_CHUNK, d_ff))
        gate = _dot(hn, wg_ref[:, cols])
        up = _dot(hn, wu_ref[:, cols])
        part = _dot(((gate * _sigmoid(gate)) * up).astype(BF16), wd_ref[cols, :])
        ffn = part if ffn is None else ffn + part
    h2 = h + ffn
    o_ref[...] = (h2 * _rms_scale(h2)) * gn_ref[...]


def _merge_ffn(x2d, rogt, aot, gl, b_gate, w_ret_out, w_att_out, w_out, norm_ffn, w_gate, w_up, w_down,
               norm_final):
    m = x2d.shape[0]
    assert w_gate.shape[1] % V7X_MXU_DIM == 0
    row = lambda width: pl.BlockSpec((ROW_TILE, width), lambda i: (i, 0))
    col = lambda height: pl.BlockSpec((height, ROW_TILE), lambda i: (0, i))
    consts = (b_gate, w_ret_out, w_att_out, w_out, norm_ffn, w_gate, w_up, w_down, norm_final)
    return pl.pallas_call(
        _merge_ffn_kernel,
        grid=(m // ROW_TILE,),
        in_specs=[row(D_MODEL), col(RET_V), col(ATT_W), row(2 * D_MODEL)] + [_const_spec(c.shape) for c in consts],
        out_specs=row(D_MODEL),
        out_shape=jax.ShapeDtypeStruct((m, D_MODEL), F32),
        compiler_params=pltpu.CompilerParams(dimension_semantics=("arbitrary",),
                                             vmem_limit_bytes=VMEM_LIMIT_BYTES),
        name="merge_ffn",
    )(x2d, rogt, aot, gl, *consts)


def _rotary_tables(seq):
    pos = jnp.arange(seq, dtype=F32)
    d = RET_KEY_DIM
    freqs = ROPE_BASE ** (-jnp.arange(0, d, 2, dtype=F32) / d)
    ang = pos[:, None] * freqs[None, :]
    cos = jnp.cos(ang)
    sin = jnp.sin(ang)
    return jnp.concatenate([cos, cos], axis=-1), jnp.concatenate([-sin, sin], axis=-1)


def _retention_tables():
    log_g = jnp.log(1.0 - 2.0 ** (-5.0 - jnp.arange(RET_HEADS, dtype=F32)))
    p = jnp.arange(Q_BLOCK, dtype=F32)
    dist = p[None, :] - p[:, None]
    chunk = np.arange(Q_BLOCK) // CHUNK
    same = jnp.asarray(chunk[:, None] == chunk[None, :])
    earlier = jnp.asarray(chunk[:, None] < chunk[None, :])
    decay = jnp.exp(log_g[:, None, None] * jnp.where(same, jnp.abs(dist), dist)[None])
    dect = jnp.where((same | earlier)[None], decay, 0.0)
    q_dec = jnp.exp(log_g[None, :] * (p[:, None] + 1.0))
    k_dec = jnp.exp(log_g[None, :] * (Q_BLOCK - 1.0 - p[:, None]))
    qdec = jnp.repeat(q_dec, RET_KEY_DIM, axis=1)
    kdec = jnp.repeat(k_dec, RET_KEY_DIM, axis=1)
    cdec = jnp.broadcast_to(jnp.exp(log_g * Q_BLOCK)[:, None, None], (RET_HEADS, 1, RET_KEY_DIM))
    return dect, qdec, kdec, cdec


def _extended_bias(rel_bias):
    left = Q_BLOCK - CHUNK
    return jnp.pad(rel_bias.astype(F32) * LOG2E,((0, 0), (left, BIAS_EXT - left - N_REL)), mode="edge")


def _visibility_caps():
    j = np.arange(KEY_WINDOW)[:, None]
    n = np.arange(Q_BLOCK)[None, :]
    in_band = (j // CHUNK >= n // CHUNK) & (j // CHUNK <= n // CHUNK + BAND_CHUNKS)
    caps = [np.where(in_band & (g * Q_BLOCK - PAD_ROWS + j >= 0), F32_MAX, NEG_INF)
            for g in range(PAD_ROWS // Q_BLOCK + 1)]
    return jnp.asarray(np.stack(caps), F32)


def _split_w_in(w_in):
    sizes = (RET_QK, RET_QK, RET_V, RET_V, ATT_W, ATT_W, ATT_W, 2 * D_MODEL)
    rq, rk, rv, rg, aq, ak, av, gl = jnp.split(w_in, np.cumsum(sizes)[:-1].tolist(), axis=1)
    w_row = jnp.concatenate([rq, rk, aq, ak, gl], axis=1).astype(BF16)
    w_t = jnp.concatenate([rv, rg, av], axis=1).T.astype(BF16)
    return w_row, w_t


def kernel(x, norm_mix, w_in, b_gate, rel_bias, w_ret_out, w_att_out, w_out, norm_ffn,
           w_ffn_gate, w_ffn_up, w_ffn_down, norm_final):
    batch, seq, d_model = x.shape
    assert w_in.shape[0] == 1, "single-layer block"
    assert d_model == D_MODEL and seq % ROW_TILE == 0 and seq % Q_BLOCK == 0 and PAD_ROWS % Q_BLOCK == 0
    cos, sin = _rotary_tables(seq)
    dect, qdec, kdec, cdec = _retention_tables()
    w_row, w_t = _split_w_in(w_in[0])
    x2d = x.reshape(batch * seq, d_model)
    rq, rk, aq, ak, gl, rvt, rgt, avt = _in_proj(x2d, norm_mix, w_row, w_t, cos, sin, seq)
    rogt, aot = _mixers(rq, rk, rvt, rgt, aq, ak, avt, _extended_bias(rel_bias[0]), _visibility_caps(),
                        dect, qdec, kdec, cdec, batch, seq)
    out = _merge_ffn(x2d, rogt, aot, gl, b_gate, w_ret_out[0].astype(BF16), w_att_out[0].astype(BF16),
                     w_out[0].astype(BF16), norm_ffn, w_ffn_gate[0].astype(BF16), w_ffn_up[0].astype(BF16),
                     w_ffn_down[0].astype(BF16), norm_final[None, :])
    return out.reshape(batch, seq, d_model)
```

```python
import jax
import jax.numpy as jnp
import numpy as np
from jax import lax
from jax.experimental import pallas as pl
from jax.experimental.pallas import tpu as pltpu

D_MODEL = 1024
CHUNK = 64
RET_HEADS = 4
RET_KEY_DIM = 128
RET_VAL_DIM = 256
RET_QK = RET_HEADS * RET_KEY_DIM
RET_V = RET_HEADS * RET_VAL_DIM
ATT_HEADS = 8
ATT_HEAD_DIM = 64
ATT_W = ATT_HEADS * ATT_HEAD_DIM
BAND_CHUNKS = 8
MAX_REL = 256
N_REL = CHUNK + MAX_REL
ROPE_BASE = 10000.0
EPS = 1e-6
NEG_INF = -1e30
F32_MAX = float(np.finfo(np.float32).max)

V7X_LANES = 128
VMEM_LIMIT_BYTES = 60 * 1024 * 1024

IN_ROW_TILE = 1024
ROW_TILE = 512
COL_CHUNK = 512
V7X_MXU_DIM = 256
FFN_CHUNK = 6 * V7X_MXU_DIM
Q_BLOCK = 4 * CHUNK
PAD_ROWS = BAND_CHUNKS * CHUNK
KEY_WINDOW = Q_BLOCK + PAD_ROWS
HALF_Q = Q_BLOCK // 2
HALF_KEYS = HALF_Q + PAD_ROWS
BIAS_EXT = 1024
BIAS_ROLL_ROWS = 128
DENOM_ROWS = 16
LOG2E = 1.4426950408889634

assert HALF_Q == V7X_LANES and KEY_WINDOW - HALF_KEYS == HALF_Q
assert BIAS_EXT >= KEY_WINDOW + Q_BLOCK - 1

BF16 = jnp.bfloat16
F32 = jnp.float32


def _const_spec(shape):
    zeros = (0,) * len(shape)
    return pl.BlockSpec(shape, lambda *_: zeros, pipeline_mode=pl.Buffered(1))


def _dot(a, b):
    return jnp.dot(a, b, preferred_element_type=F32)


def _dot_nt(a, b):
    return lax.dot_general(a, b, (((1,), (1,)), ((), ())), preferred_element_type=F32)


def _dot_tn(a, b):
    return lax.dot_general(a, b, (((0,), (0,)), ((), ())), preferred_element_type=F32)


def _sigmoid(x):
    return 1.0 / (1.0 + jnp.exp(-x))


def _rms_scale(x):
    return lax.rsqrt(jnp.mean(x * x, axis=-1, keepdims=True) + EPS)


def _in_proj_kernel(x_ref, g_ref, w_row_ref, w_t_ref, cos_ref, sin_ref,
                    rq_ref, rk_ref, aq_ref, ak_ref, gl_ref, rvt_ref, rgt_ref, avt_ref):
    x = x_ref[...]
    xn = ((x * _rms_scale(x)) * g_ref[...]).astype(BF16)
    cos = cos_ref[...]
    sin = sin_ref[...]

    def rotary_store(out_ref, col0, scale):
        t = _dot(xn, w_row_ref[:, col0:col0 + RET_QK])
        for h in range(RET_HEADS):
            cols = slice(h * RET_KEY_DIM, (h + 1) * RET_KEY_DIM)
            th = t[:, cols]
            rot = th * cos + pltpu.roll(th, RET_KEY_DIM // 2, axis=1) * sin
            out_ref[:, cols] = (rot if scale is None else rot * scale).astype(BF16)

    rotary_store(rq_ref, 0, None)
    rotary_store(rk_ref, RET_QK, RET_KEY_DIM ** -0.5)
    col = 2 * RET_QK
    for out_ref, width, scale in ((aq_ref, ATT_W, ATT_HEAD_DIM ** -0.5 * LOG2E), (ak_ref, ATT_W, None),
                                  (gl_ref, 2 * D_MODEL, None)):
        for c in range(0, width, COL_CHUNK):
            t = _dot(xn, w_row_ref[:, col + c:col + c + COL_CHUNK])
            out_ref[:, c:c + COL_CHUNK] = (t if scale is None else t * scale).astype(BF16)
        col += width
    row = 0
    for out_ref, height in ((rvt_ref, RET_V), (rgt_ref, RET_V), (avt_ref, ATT_W)):
        for r in range(0, height, COL_CHUNK):
            t = _dot_nt(w_t_ref[row + r:row + r + COL_CHUNK, :], xn)
            out_ref[r:r + COL_CHUNK, :] = t.astype(BF16)
        row += height


def _in_proj(x2d, g, w_row, w_t, cos, sin, seq):
    m = x2d.shape[0]
    steps_per_seq = seq // IN_ROW_TILE
    row = lambda width: pl.BlockSpec((IN_ROW_TILE, width), lambda i: (i, 0))
    col = lambda height: pl.BlockSpec((height, IN_ROW_TILE), lambda i: (0, i))
    pos = pl.BlockSpec((IN_ROW_TILE, RET_KEY_DIM), lambda i: (i % steps_per_seq, 0))
    row_widths = (RET_QK, RET_QK, ATT_W, ATT_W, 2 * D_MODEL)
    col_heights = (RET_V, RET_V, ATT_W)
    return pl.pallas_call(
        _in_proj_kernel,
        grid=(m // IN_ROW_TILE,),
        in_specs=[row(D_MODEL), _const_spec(g.shape), _const_spec(w_row.shape), _const_spec(w_t.shape),
                  pos, pos],
        out_specs=[row(w) for w in row_widths] + [col(h) for h in col_heights],
        out_shape=([jax.ShapeDtypeStruct((m, w), BF16) for w in row_widths]
                   + [jax.ShapeDtypeStruct((h, m), BF16) for h in col_heights]),
        compiler_params=pltpu.CompilerParams(dimension_semantics=("arbitrary",),
                                             vmem_limit_bytes=VMEM_LIMIT_BYTES),
        name="in_proj",
    )(x2d, g, w_row, w_t, cos, sin)


def _mixer_kernel(rq_ref, rk_ref, rvt_ref, rgt_ref, aq_ref, ak_ref, avt_ref,
                  ebias_ref, cap_ref, dect_ref, qdec_ref, kdec_ref, cdec_ref,
                  rogt_ref, aot_ref, kpad_ref, vtpad_ref, state_ref, biast_ref):
    b = pl.program_id(0)
    g = pl.program_id(1)
    n_pad_blocks = PAD_ROWS // Q_BLOCK

    @pl.when((b == 0) & (g == 0))
    def _():
        for h in range(ATT_HEADS):
            ext = jnp.broadcast_to(ebias_ref[h:h + 1, :], (BIAS_ROLL_ROWS, BIAS_EXT))
            for r in range(0, KEY_WINDOW, BIAS_ROLL_ROWS):
                rolled = pltpu.roll(ext, BIAS_EXT - KEY_WINDOW + 1 + r, axis=1, stride=1, stride_axis=0)
                biast_ref[h, r:r + BIAS_ROLL_ROWS, :] = rolled[:, :Q_BLOCK]

    @pl.when(g == 0)
    def _():
        kpad_ref[0:PAD_ROWS, :] = jnp.zeros((PAD_ROWS, ATT_W), BF16)
        kpad_ref[PAD_ROWS:, :] = ak_ref[...]
        for blk in range(n_pad_blocks):
            vtpad_ref[blk] = jnp.zeros((ATT_W, Q_BLOCK), BF16)
        for blk in range(avt_ref.shape[1] // Q_BLOCK):
            vtpad_ref[n_pad_blocks + blk] = avt_ref[:, blk * Q_BLOCK:(blk + 1) * Q_BLOCK]
        state_ref[...] = jnp.zeros(state_ref.shape, F32)

    start = pl.multiple_of(g * Q_BLOCK, Q_BLOCK)
    kwin = kpad_ref[pl.ds(start, KEY_WINDOW), :]
    vwin_t = jnp.concatenate([vtpad_ref[g + i] for i in range(KEY_WINDOW // Q_BLOCK)], axis=1)
    pair_w = 2 * ATT_HEAD_DIM
    low_half = lax.broadcasted_iota(jnp.int32, (Q_BLOCK, pair_w), 1) < ATT_HEAD_DIM
    no_prob = jnp.zeros((KEY_WINDOW - HALF_KEYS, HALF_Q), BF16)
    ones_rows = jnp.ones((DENOM_ROWS, KEY_WINDOW), BF16)

    def attention_scores(p):
        cols = slice(p * pair_w, (p + 1) * pair_w)
        qp = aq_ref[:, cols]
        zero = jnp.zeros_like(qp)
        q2 = jnp.concatenate([jnp.where(low_half, qp, zero), jnp.where(low_half, zero, qp)], axis=0)
        return _dot_nt(kwin[:, cols], q2)

    def attention_finish(p, st_pair):
        for e in range(2):
            h = 2 * p + e
            probs = []
            for half in range(2):
                rows = slice(half * HALF_Q, half * HALF_Q + HALF_KEYS)
                lanes = slice(half * HALF_Q, (half + 1) * HALF_Q)
                st = st_pair[rows, e * Q_BLOCK + half * HALF_Q:e * Q_BLOCK + (half + 1) * HALF_Q]
                s = jnp.minimum(st + biast_ref[h, rows, lanes], cap_ref[0, rows, lanes])
                probs.append(jnp.exp2(s - jnp.max(s, axis=0, keepdims=True)).astype(BF16))
            prob_t = jnp.concatenate([jnp.concatenate([probs[0], no_prob], axis=0),
                                      jnp.concatenate([no_prob, probs[1]], axis=0)], axis=1)
            head_rows = slice(h * ATT_HEAD_DIM, (h + 1) * ATT_HEAD_DIM)
            out_t = _dot(jnp.concatenate([vwin_t[head_rows, :], ones_rows], axis=0), prob_t)
            denom = out_t[ATT_HEAD_DIM:ATT_HEAD_DIM + 1, :]
            aot_ref[head_rows, :] = (out_t[:ATT_HEAD_DIM, :] * (1.0 / denom)).astype(BF16)

    def retention_dots(h):
        kcols = slice(h * RET_KEY_DIM, (h + 1) * RET_KEY_DIM)
        q = rq_ref[:, kcols]
        k = rk_ref[:, kcols]
        vt = rvt_ref[h * RET_VAL_DIM:(h + 1) * RET_VAL_DIM, :]
        state_t = state_ref[h]
        qd = (q.astype(F32) * qdec_ref[:, kcols]).astype(BF16)
        kd = (k.astype(F32) * kdec_ref[:, kcols]).astype(BF16)
        scores_t = _dot_nt(k, q)
        carried = _dot_nt(state_t.astype(BF16), qd)
        state_ref[h] = state_t * cdec_ref[h] + _dot(vt, kd)
        return scores_t, carried

    def retention_finish(h, scores_t, carried):
        vrows = slice(h * RET_VAL_DIM, (h + 1) * RET_VAL_DIM)
        out_t = _dot(rvt_ref[vrows, :], (scores_t * dect_ref[h]).astype(BF16)) + carried
        mu = jnp.mean(out_t, axis=0, keepdims=True)
        dev = out_t - mu
        var = jnp.mean(dev * dev, axis=0, keepdims=True)
        gate = rgt_ref[vrows, :].astype(F32)
        rogt_ref[vrows, :] = ((gate * _sigmoid(gate)) * (dev * lax.rsqrt(var + EPS))).astype(BF16)

    assert ATT_HEADS // 2 == RET_HEADS
    att_next = attention_scores(0)
    ret_next = retention_dots(0)
    for i in range(RET_HEADS):
        att_now, ret_now = att_next, ret_next
        if i + 1 < RET_HEADS:
            att_next = attention_scores(i + 1)
            ret_next = retention_dots(i + 1)
        attention_finish(i, att_now)
        retention_finish(i, *ret_now)


def _mixers(rq, rk, rvt, rgt, aq, ak, avt, ebias, cap, dect, qdec, kdec, cdec, batch, seq):
    m = rq.shape[0]
    blocks_per_seq = seq // Q_BLOCK
    n_cap = cap.shape[0]
    qrow = lambda width: pl.BlockSpec((Q_BLOCK, width), lambda b, g: (b * blocks_per_seq + g, 0))
    qcol = lambda height: pl.BlockSpec((height, Q_BLOCK), lambda b, g: (0, b * blocks_per_seq + g))
    cap_spec = pl.BlockSpec((1, KEY_WINDOW, Q_BLOCK), lambda b, g: (jnp.minimum(g, n_cap - 1), 0, 0))
    return pl.pallas_call(
        _mixer_kernel,
        grid=(batch, blocks_per_seq),
        in_specs=[qrow(RET_QK), qrow(RET_QK), qcol(RET_V), qcol(RET_V), qrow(ATT_W),
                  pl.BlockSpec((seq, ATT_W), lambda b, g: (b, 0)),
                  pl.BlockSpec((ATT_W, seq), lambda b, g: (0, b)),
                  _const_spec(ebias.shape), cap_spec, _const_spec(dect.shape),
                  _const_spec(qdec.shape), _const_spec(kdec.shape), _const_spec(cdec.shape)],
        out_specs=[qcol(RET_V), qcol(ATT_W)],
        out_shape=[jax.ShapeDtypeStruct((RET_V, m), BF16), jax.ShapeDtypeStruct((ATT_W, m), BF16)],
        scratch_shapes=[pltpu.VMEM((PAD_ROWS + seq, ATT_W), BF16),
                        pltpu.VMEM(((PAD_ROWS + seq) // Q_BLOCK, ATT_W, Q_BLOCK), BF16),
                        pltpu.VMEM((RET_HEADS, RET_VAL_DIM, RET_KEY_DIM), F32),
                        pltpu.VMEM((ATT_HEADS, KEY_WINDOW, Q_BLOCK), F32)],
        compiler_params=pltpu.CompilerParams(dimension_semantics=("arbitrary", "arbitrary"),
                                             vmem_limit_bytes=VMEM_LIMIT_BYTES),
        name="mixers",
    )(rq, rk, rvt, rgt, aq, ak, avt, ebias, cap, dect, qdec, kdec, cdec)


def _merge_ffn_kernel(x_ref, rogt_ref, aot_ref, gl_ref, bg_ref, wro_ref, wao_ref, wout_ref,
                      gf_ref, wg_ref, wu_ref, wd_ref, gn_ref, o_ref):
    y_ret = _dot_tn(rogt_ref[...], wro_ref[...])
    y_att = _dot_tn(aot_ref[...], wao_ref[...])
    g_ret = _sigmoid(gl_ref[:, :D_MODEL].astype(F32) + bg_ref[:, :D_MODEL])
    g_att = _sigmoid(gl_ref[:, D_MODEL:].astype(F32) + bg_ref[:, D_MODEL:])
    mixed = (g_ret * y_ret + g_att * y_att).astype(BF16)
    h = x_ref[...] + _dot(mixed, wout_ref[...])
    hn = ((h * _rms_scale(h)) * gf_ref[...]).astype(BF16)
    d_ff = wg_ref.shape[1]
    ffn = None
    for c in range(0, d_ff, FFN_CHUNK):
        cols = slice(c, min(c + FFN_CHUNK, d_ff))
        gate = _dot(hn, wg_ref[:, cols])
        up = _dot(hn, wu_ref[:, cols])
        part = _dot(((gate * _sigmoid(gate)) * up).astype(BF16), wd_ref[cols, :])
        ffn = part if ffn is None else ffn + part
    h2 = h + ffn
    o_ref[...] = (h2 * _rms_scale(h2)) * gn_ref[...]


def _merge_ffn(x2d, rogt, aot, gl, b_gate, w_ret_out, w_att_out, w_out, norm_ffn, w_gate, w_up, w_down,
               norm_final):
    m = x2d.shape[0]
    assert w_gate.shape[1] % V7X_MXU_DIM == 0
    row = lambda width: pl.BlockSpec((ROW_TILE, width), lambda i: (i, 0))
    col = lambda height: pl.BlockSpec((height, ROW_TILE), lambda i: (0, i))
    consts = (b_gate, w_ret_out, w_att_out, w_out, norm_ffn, w_gate, w_up, w_down, norm_final)
    return pl.pallas_call(
        _merge_ffn_kernel,
        grid=(m // ROW_TILE,),
        in_specs=[row(D_MODEL), col(RET_V), col(ATT_W), row(2 * D_MODEL)] + [_const_spec(c.shape) for c in consts],
        out_specs=row(D_MODEL),
        out_shape=jax.ShapeDtypeStruct((m, D_MODEL), F32),
        compiler_params=pltpu.CompilerParams(dimension_semantics=("arbitrary",),
                                             vmem_limit_bytes=VMEM_LIMIT_BYTES),
        name="merge_ffn",
    )(x2d, rogt, aot, gl, *consts)


def _rotary_tables(seq):
    d = RET_KEY_DIM
    freqs = ROPE_BASE ** (-np.arange(0, d, 2, dtype=np.float64) / d)
    ang = np.arange(seq, dtype=np.float64)[:, None] * freqs[None, :]
    cos = np.cos(ang)
    sin = np.sin(ang)
    return (jnp.asarray(np.concatenate([cos, cos], axis=-1), F32),
            jnp.asarray(np.concatenate([-sin, sin], axis=-1), F32))


def _retention_tables():
    log_g = np.log(1.0 - 2.0 ** (-5.0 - np.arange(RET_HEADS, dtype=np.float64)))
    p = np.arange(Q_BLOCK, dtype=np.float64)
    dist = p[None, :] - p[:, None]
    chunk = np.arange(Q_BLOCK) // CHUNK
    same = chunk[:, None] == chunk[None, :]
    earlier = chunk[:, None] < chunk[None, :]
    decay = np.exp(log_g[:, None, None] * np.where(same, np.abs(dist), dist)[None])
    dect = np.where((same | earlier)[None], decay, 0.0)
    q_dec = np.exp(log_g[None, :] * (p[:, None] + 1.0))
    k_dec = np.exp(log_g[None, :] * (Q_BLOCK - 1.0 - p[:, None]))
    qdec = np.repeat(q_dec, RET_KEY_DIM, axis=1)
    kdec = np.repeat(k_dec, RET_KEY_DIM, axis=1)
    cdec = np.broadcast_to(np.exp(log_g * Q_BLOCK)[:, None, None], (RET_HEADS, 1, RET_KEY_DIM))
    return tuple(jnp.asarray(t, F32) for t in (dect, qdec, kdec, cdec))


def _extended_bias(rel_bias):
    left = Q_BLOCK - CHUNK
    return jnp.pad(rel_bias.astype(F32) * LOG2E, ((0, 0), (left, BIAS_EXT - left - N_REL)), mode="edge")


def _visibility_caps():
    j = np.arange(KEY_WINDOW)[:, None]
    n = np.arange(Q_BLOCK)[None, :]
    in_band = (j // CHUNK >= n // CHUNK) & (j // CHUNK <= n // CHUNK + BAND_CHUNKS)
    caps = [np.where(in_band & (g * Q_BLOCK - PAD_ROWS + j >= 0), F32_MAX, NEG_INF)
            for g in range(PAD_ROWS // Q_BLOCK + 1)]
    return jnp.asarray(np.stack(caps), F32)


def _split_w_in(w_in):
    sizes = (RET_QK, RET_QK, RET_V, RET_V, ATT_W, ATT_W, ATT_W, 2 * D_MODEL)
    rq, rk, rv, rg, aq, ak, av, gl = jnp.split(w_in, np.cumsum(sizes)[:-1].tolist(), axis=1)
    w_row = jnp.concatenate([rq, rk, aq, ak, gl], axis=1).astype(BF16)
    w_t = jnp.concatenate([rv, rg, av], axis=1).T.astype(BF16)
    return w_row, w_t


def kernel(x, norm_mix, w_in, b_gate, rel_bias, w_ret_out, w_att_out, w_out, norm_ffn,
           w_ffn_gate, w_ffn_up, w_ffn_down, norm_final):
    batch, seq, d_model = x.shape
    assert w_in.shape[0] == 1, "single-layer block"
    assert d_model == D_MODEL and seq % ROW_TILE == 0 and seq % IN_ROW_TILE == 0 and seq % Q_BLOCK == 0 and PAD_ROWS % Q_BLOCK == 0
    cos, sin = _rotary_tables(seq)
    dect, qdec, kdec, cdec = _retention_tables()
    w_row, w_t = _split_w_in(w_in[0])
    x2d = x.reshape(batch * seq, d_model)
    rq, rk, aq, ak, gl, rvt, rgt, avt = _in_proj(x2d, norm_mix, w_row, w_t, cos, sin, seq)
    rogt, aot = _mixers(rq, rk, rvt, rgt, aq, ak, avt, _extended_bias(rel_bias[0]), _visibility_caps(),
                        dect, qdec, kdec, cdec, batch, seq)
    out = _merge_ffn(x2d, rogt, aot, gl, b_gate, w_ret_out[0].astype(BF16), w_att_out[0].astype(BF16),
                     w_out[0].astype(BF16), norm_ffn, w_ffn_gate[0].astype(BF16), w_ffn_up[0].astype(BF16),
                     w_ffn_down[0].astype(BF16), norm_final[None, :])
    return out.reshape(batch, seq, d_model)
```

```python
import jax
import jax.numpy as jnp
import numpy as np
from jax import lax
from jax.experimental import pallas as pl
from jax.experimental.pallas import tpu as pltpu

D_MODEL = 1024
CHUNK = 64
RET_HEADS = 4
RET_KEY_DIM = 128
RET_VAL_DIM = 256
RET_QK = RET_HEADS * RET_KEY_DIM
RET_V = RET_HEADS * RET_VAL_DIM
ATT_HEADS = 8
ATT_HEAD_DIM = 64
ATT_W = ATT_HEADS * ATT_HEAD_DIM
BAND_CHUNKS = 8
MAX_REL = 256
N_REL = CHUNK + MAX_REL
ROPE_BASE = 10000.0
EPS = 1e-6
NEG_INF = -1e30
F32_MAX = float(np.finfo(np.float32).max)

V7X_LANES = 128
VMEM_LIMIT_BYTES = 60 * 1024 * 1024

IN_ROW_TILE = 1024
ROW_TILE = 512
COL_CHUNK = 512
V7X_MXU_DIM = 256
FFN_CHUNK = 6 * V7X_MXU_DIM
Q_BLOCK = 4 * CHUNK
STEP_BLOCKS = 2
PAD_ROWS = BAND_CHUNKS * CHUNK
KEY_WINDOW = Q_BLOCK + PAD_ROWS
HALF_Q = Q_BLOCK // 2
HALF_KEYS = HALF_Q + PAD_ROWS
BIAS_EXT = 1024
BIAS_ROLL_ROWS = 128
DENOM_ROWS = 16
LOG2E = 1.4426950408889634

assert HALF_Q == V7X_LANES and KEY_WINDOW - HALF_KEYS == HALF_Q
assert BIAS_EXT >= KEY_WINDOW + Q_BLOCK - 1

BF16 = jnp.bfloat16
F32 = jnp.float32


def _const_spec(shape):
    zeros = (0,) * len(shape)
    return pl.BlockSpec(shape, lambda *_: zeros, pipeline_mode=pl.Buffered(1))


def _dot(a, b):
    return jnp.dot(a, b, preferred_element_type=F32)


def _dot_nt(a, b):
    return lax.dot_general(a, b, (((1,), (1,)), ((), ())), preferred_element_type=F32)


def _dot_tn(a, b):
    return lax.dot_general(a, b, (((0,), (0,)), ((), ())), preferred_element_type=F32)


def _sigmoid(x):
    return 1.0 / (1.0 + jnp.exp(-x))


def _rms_scale(x):
    return lax.rsqrt(jnp.mean(x * x, axis=-1, keepdims=True) + EPS)


def _in_proj_kernel(x_ref, g_ref, w_row_ref, w_t_ref, cos_ref, sin_ref,
                    rq_ref, rk_ref, aq_ref, ak_ref, gl_ref, rvt_ref, rgt_ref, avt_ref):
    x = x_ref[...]
    xn = ((x * _rms_scale(x)) * g_ref[...]).astype(BF16)
    cos = cos_ref[...]
    sin = sin_ref[...]

    def rotary_store(out_ref, col0, scale):
        t = _dot(xn, w_row_ref[:, col0:col0 + RET_QK])
        for h in range(RET_HEADS):
            cols = slice(h * RET_KEY_DIM, (h + 1) * RET_KEY_DIM)
            th = t[:, cols]
            rot = th * cos + pltpu.roll(th, RET_KEY_DIM // 2, axis=1) * sin
            out_ref[:, cols] = (rot if scale is None else rot * scale).astype(BF16)

    rotary_store(rq_ref, 0, None)
    rotary_store(rk_ref, RET_QK, RET_KEY_DIM ** -0.5)
    col = 2 * RET_QK
    for out_ref, width, scale in ((aq_ref, ATT_W, ATT_HEAD_DIM ** -0.5 * LOG2E), (ak_ref, ATT_W, None),
                                  (gl_ref, 2 * D_MODEL, None)):
        for c in range(0, width, COL_CHUNK):
            t = _dot(xn, w_row_ref[:, col + c:col + c + COL_CHUNK])
            out_ref[:, c:c + COL_CHUNK] = (t if scale is None else t * scale).astype(BF16)
        col += width
    row = 0
    for out_ref, height in ((rvt_ref, RET_V), (rgt_ref, RET_V), (avt_ref, ATT_W)):
        for r in range(0, height, COL_CHUNK):
            t = _dot_nt(w_t_ref[row + r:row + r + COL_CHUNK, :], xn)
            out_ref[r:r + COL_CHUNK, :] = t.astype(BF16)
        row += height


def _in_proj(x2d, g, w_row, w_t, cos, sin, seq):
    m = x2d.shape[0]
    steps_per_seq = seq // IN_ROW_TILE
    row = lambda width: pl.BlockSpec((IN_ROW_TILE, width), lambda i: (i, 0))
    col = lambda height: pl.BlockSpec((height, IN_ROW_TILE), lambda i: (0, i))
    pos = pl.BlockSpec((IN_ROW_TILE, RET_KEY_DIM), lambda i: (i % steps_per_seq, 0))
    row_widths = (RET_QK, RET_QK, ATT_W, ATT_W, 2 * D_MODEL)
    col_heights = (RET_V, RET_V, ATT_W)
    return pl.pallas_call(
        _in_proj_kernel,
        grid=(m // IN_ROW_TILE,),
        in_specs=[row(D_MODEL), _const_spec(g.shape), _const_spec(w_row.shape), _const_spec(w_t.shape),
                  pos, pos],
        out_specs=[row(w) for w in row_widths] + [col(h) for h in col_heights],
        out_shape=([jax.ShapeDtypeStruct((m, w), BF16) for w in row_widths]
                   + [jax.ShapeDtypeStruct((h, m), BF16) for h in col_heights]),
        compiler_params=pltpu.CompilerParams(dimension_semantics=("arbitrary",),
                                             vmem_limit_bytes=VMEM_LIMIT_BYTES),
        name="in_proj",
    )(x2d, g, w_row, w_t, cos, sin)


def _mixer_kernel(rq_ref, rk_ref, rvt_ref, rgt_ref, aq_ref, ak_ref, avt_ref,
                  ebias_ref, cap_ref, dect_ref, qdec_ref, kdec_ref, cdec_ref,
                  rogt_ref, aot_ref, kpad_ref, vtpad_ref, state_ref, biast_ref):
    b = pl.program_id(0)
    g = pl.program_id(1)
    n_pad_blocks = PAD_ROWS // Q_BLOCK

    @pl.when((b == 0) & (g == 0))
    def _():
        for h in range(ATT_HEADS):
            ext = jnp.broadcast_to(ebias_ref[h:h + 1, :], (BIAS_ROLL_ROWS, BIAS_EXT))
            for r in range(0, KEY_WINDOW, BIAS_ROLL_ROWS):
                rolled = pltpu.roll(ext, BIAS_EXT - KEY_WINDOW + 1 + r, axis=1, stride=1, stride_axis=0)
                biast_ref[h, r:r + BIAS_ROLL_ROWS, :] = rolled[:, :Q_BLOCK]

    @pl.when(g == 0)
    def _():
        kpad_ref[0:PAD_ROWS, :] = jnp.zeros((PAD_ROWS, ATT_W), BF16)
        kpad_ref[PAD_ROWS:, :] = ak_ref[...]
        for blk in range(n_pad_blocks):
            vtpad_ref[blk] = jnp.zeros((ATT_W, Q_BLOCK), BF16)
        for blk in range(avt_ref.shape[1] // Q_BLOCK):
            vtpad_ref[n_pad_blocks + blk] = avt_ref[:, blk * Q_BLOCK:(blk + 1) * Q_BLOCK]
        state_ref[...] = jnp.zeros(state_ref.shape, F32)

    pair_w = 2 * ATT_HEAD_DIM
    low_lanes = lax.broadcasted_iota(jnp.int32, (HALF_Q, pair_w), 1) < ATT_HEAD_DIM
    first_chunk_q = lax.broadcasted_iota(jnp.int32, (CHUNK, HALF_Q), 1) < CHUNK
    no_prob = jnp.zeros((KEY_WINDOW - HALF_KEYS, HALF_Q), BF16)
    ones_rows = jnp.ones((DENOM_ROWS, KEY_WINDOW), BF16)
    half_rows = [slice(half * HALF_Q, half * HALF_Q + HALF_KEYS) for half in range(2)]

    def run_blocks(step_block0, cap_blocks):
        def attention_scores(blk, p):
            cols = slice(p * pair_w, (p + 1) * pair_w)
            start = pl.multiple_of((step_block0 + blk) * Q_BLOCK, Q_BLOCK)
            kwin = kpad_ref[pl.ds(start, KEY_WINDOW), cols]
            scores = []
            for half in range(2):
                qrows = slice(blk * Q_BLOCK + half * HALF_Q, blk * Q_BLOCK + (half + 1) * HALF_Q)
                qp = aq_ref[qrows, cols]
                zero = jnp.zeros_like(qp)
                q2 = jnp.concatenate([jnp.where(low_lanes, qp, zero), jnp.where(low_lanes, zero, qp)], axis=0)
                scores.append(_dot_nt(kwin[half_rows[half], :], q2))
            return scores

        def attention_finish(blk, p, scores):
            vwin_t = jnp.concatenate([vtpad_ref[step_block0 + blk + i] for i in range(KEY_WINDOW // Q_BLOCK)],
                                     axis=1)
            out_lanes = slice(blk * Q_BLOCK, (blk + 1) * Q_BLOCK)
            for e in range(2):
                h = 2 * p + e
                probs = []
                for half in range(2):
                    lanes = slice(half * HALF_Q, (half + 1) * HALF_Q)
                    s = scores[half][:, e * HALF_Q:(e + 1) * HALF_Q] + biast_ref[h, half_rows[half], lanes]
                    if cap_blocks[blk] is None:
                        s = jnp.concatenate([jnp.where(first_chunk_q, s[:CHUNK], NEG_INF),
                                             s[CHUNK:HALF_KEYS - CHUNK],
                                             jnp.where(first_chunk_q, NEG_INF, s[HALF_KEYS - CHUNK:])], axis=0)
                    else:
                        s = jnp.minimum(s, cap_ref[cap_blocks[blk], half_rows[half], lanes])
                    probs.append(jnp.exp2(s - jnp.max(s, axis=0, keepdims=True)).astype(BF16))
                prob_t = jnp.concatenate([jnp.concatenate([probs[0], no_prob], axis=0),
                                          jnp.concatenate([no_prob, probs[1]], axis=0)], axis=1)
                head_rows = slice(h * ATT_HEAD_DIM, (h + 1) * ATT_HEAD_DIM)
                out_t = _dot(jnp.concatenate([vwin_t[head_rows, :], ones_rows], axis=0), prob_t)
                denom = out_t[ATT_HEAD_DIM:ATT_HEAD_DIM + 1, :]
                aot_ref[head_rows, out_lanes] = (out_t[:ATT_HEAD_DIM, :] * (1.0 / denom)).astype(BF16)

        def retention_dots(blk, h):
            kcols = slice(h * RET_KEY_DIM, (h + 1) * RET_KEY_DIM)
            pos = slice(blk * Q_BLOCK, (blk + 1) * Q_BLOCK)
            q = rq_ref[pos, kcols]
            k = rk_ref[pos, kcols]
            vt = rvt_ref[h * RET_VAL_DIM:(h + 1) * RET_VAL_DIM, pos]
            state_t = state_ref[h]
            qd = (q.astype(F32) * qdec_ref[:, kcols]).astype(BF16)
            kd = (k.astype(F32) * kdec_ref[:, kcols]).astype(BF16)
            scores_t = _dot_nt(k, q)
            carried = _dot_nt(state_t.astype(BF16), qd)
            state_ref[h] = state_t * cdec_ref[h] + _dot(vt, kd)
            return scores_t, carried

        def retention_finish(blk, h, scores_t, carried):
            vrows = slice(h * RET_VAL_DIM, (h + 1) * RET_VAL_DIM)
            pos = slice(blk * Q_BLOCK, (blk + 1) * Q_BLOCK)
            out_t = _dot(rvt_ref[vrows, pos], (scores_t * dect_ref[h]).astype(BF16)) + carried
            mu = jnp.mean(out_t, axis=0, keepdims=True)
            dev = out_t - mu
            var = jnp.mean(dev * dev, axis=0, keepdims=True)
            gate = rgt_ref[vrows, pos].astype(F32)
            rogt_ref[vrows, pos] = ((gate * _sigmoid(gate)) * (dev * lax.rsqrt(var + EPS))).astype(BF16)

        assert ATT_HEADS // 2 == RET_HEADS
        items = [(blk, i) for blk in range(len(cap_blocks)) for i in range(RET_HEADS)]
        lead = lambda blk, i: (attention_scores(blk, i), retention_dots(blk, i))
        nxt = lead(*items[0])
        for idx, (blk, i) in enumerate(items):
            now = nxt
            if idx + 1 < len(items):
                nxt = lead(*items[idx + 1])
            attention_finish(blk, i, now[0])
            retention_finish(blk, i, *now[1])

    assert n_pad_blocks == STEP_BLOCKS

    @pl.when(g == 0)
    def _():
        run_blocks(0, list(range(STEP_BLOCKS)))

    @pl.when(g > 0)
    def _():
        run_blocks(g * STEP_BLOCKS, [None] * STEP_BLOCKS)


def _mixers(rq, rk, rvt, rgt, aq, ak, avt, ebias, cap, dect, qdec, kdec, cdec, batch, seq):
    m = rq.shape[0]
    q_step = STEP_BLOCKS * Q_BLOCK
    steps_per_seq = seq // q_step
    qrow = lambda width: pl.BlockSpec((q_step, width), lambda b, g: (b * steps_per_seq + g, 0))
    qcol = lambda height: pl.BlockSpec((height, q_step), lambda b, g: (0, b * steps_per_seq + g))
    return pl.pallas_call(
        _mixer_kernel,
        grid=(batch, steps_per_seq),
        in_specs=[qrow(RET_QK), qrow(RET_QK), qcol(RET_V), qcol(RET_V), qrow(ATT_W),
                  pl.BlockSpec((seq, ATT_W), lambda b, g: (b, 0)),
                  pl.BlockSpec((ATT_W, seq), lambda b, g: (0, b)),
                  _const_spec(ebias.shape), _const_spec(cap.shape), _const_spec(dect.shape),
                  _const_spec(qdec.shape), _const_spec(kdec.shape), _const_spec(cdec.shape)],
        out_specs=[qcol(RET_V), qcol(ATT_W)],
        out_shape=[jax.ShapeDtypeStruct((RET_V, m), BF16), jax.ShapeDtypeStruct((ATT_W, m), BF16)],
        scratch_shapes=[pltpu.VMEM((PAD_ROWS + seq, ATT_W), BF16),
                        pltpu.VMEM(((PAD_ROWS + seq) // Q_BLOCK, ATT_W, Q_BLOCK), BF16),
                        pltpu.VMEM((RET_HEADS, RET_VAL_DIM, RET_KEY_DIM), F32),
                        pltpu.VMEM((ATT_HEADS, KEY_WINDOW, Q_BLOCK), F32)],
        compiler_params=pltpu.CompilerParams(dimension_semantics=("arbitrary", "arbitrary"),
                                             vmem_limit_bytes=VMEM_LIMIT_BYTES),
        name="mixers",
    )(rq, rk, rvt, rgt, aq, ak, avt, ebias, cap, dect, qdec, kdec, cdec)


def _merge_ffn_kernel(x_ref, rogt_ref, aot_ref, gl_ref, bg_ref, wro_ref, wao_ref, wout_ref,
                      gf_ref, wg_ref, wu_ref, wd_ref, gn_ref, o_ref):
    y_ret = _dot_tn(rogt_ref[...], wro_ref[...])
    y_att = _dot_tn(aot_ref[...], wao_ref[...])
    g_ret = _sigmoid(gl_ref[:, :D_MODEL].astype(F32) + bg_ref[:, :D_MODEL])
    g_att = _sigmoid(gl_ref[:, D_MODEL:].astype(F32) + bg_ref[:, D_MODEL:])
    mixed = (g_ret * y_ret + g_att * y_att).astype(BF16)
    h = x_ref[...] + _dot(mixed, wout_ref[...])
    hn = ((h * _rms_scale(h)) * gf_ref[...]).astype(BF16)
    d_ff = wg_ref.shape[1]
    ffn = None
    for c in range(0, d_ff, FFN_CHUNK):
        cols = slice(c, min(c + FFN_CHUNK, d_ff))
        gate = _dot(hn, wg_ref[:, cols])
        up = _dot(hn, wu_ref[:, cols])
        part = _dot(((gate * _sigmoid(gate)) * up).astype(BF16), wd_ref[cols, :])
        ffn = part if ffn is None else ffn + part
    h2 = h + ffn
    o_ref[...] = (h2 * _rms_scale(h2)) * gn_ref[...]


def _merge_ffn(x2d, rogt, aot, gl, b_gate, w_ret_out, w_att_out, w_out, norm_ffn, w_gate, w_up, w_down,
               norm_final):
    m = x2d.shape[0]
    assert w_gate.shape[1] % V7X_MXU_DIM == 0
    row = lambda width: pl.BlockSpec((ROW_TILE, width), lambda i: (i, 0))
    col = lambda height: pl.BlockSpec((height, ROW_TILE), lambda i: (0, i))
    consts = (b_gate, w_ret_out, w_att_out, w_out, norm_ffn, w_gate, w_up, w_down, norm_final)
    return pl.pallas_call(
        _merge_ffn_kernel,
        grid=(m // ROW_TILE,),
        in_specs=[row(D_MODEL), col(RET_V), col(ATT_W), row(2 * D_MODEL)] + [_const_spec(c.shape) for c in consts],
        out_specs=row(D_MODEL),
        out_shape=jax.ShapeDtypeStruct((m, D_MODEL), F32),
        compiler_params=pltpu.CompilerParams(dimension_semantics=("arbitrary",),
                                             vmem_limit_bytes=VMEM_LIMIT_BYTES),
        name="merge_ffn",
    )(x2d, rogt, aot, gl, *consts)


def _rotary_tables(seq):
    d = RET_KEY_DIM
    freqs = ROPE_BASE ** (-np.arange(0, d, 2, dtype=np.float64) / d)
    ang = np.arange(seq, dtype=np.float64)[:, None] * freqs[None, :]
    cos = np.cos(ang)
    sin = np.sin(ang)
    return (jnp.asarray(np.concatenate([cos, cos], axis=-1), F32),
            jnp.asarray(np.concatenate([-sin, sin], axis=-1), F32))


def _retention_tables():
    log_g = np.log(1.0 - 2.0 ** (-5.0 - np.arange(RET_HEADS, dtype=np.float64)))
    p = np.arange(Q_BLOCK, dtype=np.float64)
    dist = p[None, :] - p[:, None]
    chunk = np.arange(Q_BLOCK) // CHUNK
    same = chunk[:, None] == chunk[None, :]
    earlier = chunk[:, None] < chunk[None, :]
    decay = np.exp(log_g[:, None, None] * np.where(same, np.abs(dist), dist)[None])
    dect = np.where((same | earlier)[None], decay, 0.0)
    q_dec = np.exp(log_g[None, :] * (p[:, None] + 1.0))
    k_dec = np.exp(log_g[None, :] * (Q_BLOCK - 1.0 - p[:, None]))
    qdec = np.repeat(q_dec, RET_KEY_DIM, axis=1)
    kdec = np.repeat(k_dec, RET_KEY_DIM, axis=1)
    cdec = np.broadcast_to(np.exp(log_g * Q_BLOCK)[:, None, None], (RET_HEADS, 1, RET_KEY_DIM))
    return tuple(jnp.asarray(t, F32) for t in (dect, qdec, kdec, cdec))


def _extended_bias(rel_bias):
    left = Q_BLOCK - CHUNK
    return jnp.pad(rel_bias.astype(F32) * LOG2E, ((0, 0), (left, BIAS_EXT - left - N_REL)), mode="edge")


def _visibility_caps():
    j = np.arange(KEY_WINDOW)[:, None]
    n = np.arange(Q_BLOCK)[None, :]
    in_band = (j // CHUNK >= n // CHUNK) & (j // CHUNK <= n // CHUNK + BAND_CHUNKS)
    caps = [np.where(in_band & (g * Q_BLOCK - PAD_ROWS + j >= 0), F32_MAX, NEG_INF)
            for g in range(PAD_ROWS // Q_BLOCK)]
    return jnp.asarray(np.stack(caps), F32)


def _split_w_in(w_in):
    sizes = (RET_QK, RET_QK, RET_V, RET_V, ATT_W, ATT_W, ATT_W, 2 * D_MODEL)
    rq, rk, rv, rg, aq, ak, av, gl = jnp.split(w_in, np.cumsum(sizes)[:-1].tolist(), axis=1)
    w_row = jnp.concatenate([rq, rk, aq, ak, gl], axis=1).astype(BF16)
    w_t = jnp.concatenate([rv, rg, av], axis=1).T.astype(BF16)
    return w_row, w_t


def kernel(x, norm_mix, w_in, b_gate, rel_bias, w_ret_out, w_att_out, w_out, norm_ffn,
           w_ffn_gate, w_ffn_up, w_ffn_down, norm_final):
    batch, seq, d_model = x.shape
    assert w_in.shape[0] == 1, "single-layer block"
    assert d_model == D_MODEL and seq % ROW_TILE == 0 and seq % IN_ROW_TILE == 0
    assert seq % (STEP_BLOCKS * Q_BLOCK) == 0 and PAD_ROWS == STEP_BLOCKS * Q_BLOCK
    cos, sin = _rotary_tables(seq)
    dect, qdec, kdec, cdec = _retention_tables()
    w_row, w_t = _split_w_in(w_in[0])
    x2d = x.reshape(batch * seq, d_model)
    rq, rk, aq, ak, gl, rvt, rgt, avt = _in_proj(x2d, norm_mix, w_row, w_t, cos, sin, seq)
    rogt, aot = _mixers(rq, rk, rvt, rgt, aq, ak, avt, _extended_bias(rel_bias[0]), _visibility_caps(),
                        dect, qdec, kdec, cdec, batch, seq)
    out = _merge_ffn(x2d, rogt, aot, gl, b_gate, w_ret_out[0].astype(BF16), w_att_out[0].astype(BF16),
                     w_out[0].astype(BF16), norm_ffn, w_ffn_gate[0].astype(BF16), w_ffn_up[0].astype(BF16),
                     w_ffn_down[0].astype(BF16), norm_final[None, :])
    return out.reshape(batch, seq, d_model)
```

```python
import jax
import jax.numpy as jnp
import numpy as np
from jax import lax
from jax.experimental import pallas as pl
from jax.experimental.pallas import tpu as pltpu

D_MODEL = 1024
CHUNK = 64
RET_HEADS = 4
RET_KEY_DIM = 128
RET_VAL_DIM = 256
RET_QK = RET_HEADS * RET_KEY_DIM
RET_V = RET_HEADS * RET_VAL_DIM
ATT_HEADS = 8
ATT_HEAD_DIM = 64
ATT_W = ATT_HEADS * ATT_HEAD_DIM
BAND_CHUNKS = 8
MAX_REL = 256
N_REL = CHUNK + MAX_REL
ROPE_BASE = 10000.0
EPS = 1e-6
NEG_INF = -1e30
F32_MAX = float(np.finfo(np.float32).max)

V7X_LANES = 128
V7X_MXU_DIM = 256
VMEM_LIMIT_BYTES = 60 * 1024 * 1024

IN_ROW_TILE = 1024
ROW_TILE = 512
COL_CHUNK = 512
FFN_CHUNK = 6 * V7X_MXU_DIM
Q_BLOCK = 4 * CHUNK
STEP_BLOCKS = 2
PAD_ROWS = BAND_CHUNKS * CHUNK
KEY_WINDOW = Q_BLOCK + PAD_ROWS
HALF_Q = Q_BLOCK // 2
HALF_KEYS = HALF_Q + PAD_ROWS
BIAS_EXT = 1024
BIAS_ROLL_ROWS = 128
DENOM_ROWS = 16
LOG2E = 1.4426950408889634

assert HALF_Q == V7X_LANES and KEY_WINDOW - HALF_KEYS == HALF_Q
assert BIAS_EXT >= KEY_WINDOW + Q_BLOCK - 1

BF16 = jnp.bfloat16
F32 = jnp.float32


def _const_spec(shape):
    zeros = (0,) * len(shape)
    return pl.BlockSpec(shape, lambda *_: zeros, pipeline_mode=pl.Buffered(1))


def _dot(a, b):
    return jnp.dot(a, b, preferred_element_type=F32)


def _dot_nt(a, b):
    return lax.dot_general(a, b, (((1,), (1,)), ((), ())), preferred_element_type=F32)


def _dot_tn(a, b):
    return lax.dot_general(a, b, (((0,), (0,)), ((), ())), preferred_element_type=F32)


def _silu(x):
    u = 0.5 * x
    return u + u * jnp.tanh(u)


def _rms_scale(x):
    return lax.rsqrt(jnp.mean(x * x, axis=-1, keepdims=True) + EPS)


def _in_proj_kernel(x_ref, g_ref, w_row_ref, w_t_ref, cos_ref, sin_ref,
                    rq_ref, rk_ref, aq_ref, ak_ref, gl_ref, rvt_ref, rgt_ref, avt_ref):
    x = x_ref[...]
    xn = ((x * _rms_scale(x)) * g_ref[...]).astype(BF16)
    cos = cos_ref[...]
    sin = sin_ref[...]

    def rotary_store(out_ref, col0, scale):
        t = _dot(xn, w_row_ref[:, col0:col0 + RET_QK])
        for h in range(RET_HEADS):
            cols = slice(h * RET_KEY_DIM, (h + 1) * RET_KEY_DIM)
            th = t[:, cols]
            rot = th * cos + pltpu.roll(th, RET_KEY_DIM // 2, axis=1) * sin
            out_ref[:, cols] = (rot if scale is None else rot * scale).astype(BF16)

    rotary_store(rq_ref, 0, None)
    rotary_store(rk_ref, RET_QK, RET_KEY_DIM ** -0.5)
    col = 2 * RET_QK
    for out_ref, width, scale in ((aq_ref, ATT_W, ATT_HEAD_DIM ** -0.5 * LOG2E), (ak_ref, ATT_W, None),
                                  (gl_ref, 2 * D_MODEL, 0.5)):
        for c in range(0, width, COL_CHUNK):
            t = _dot(xn, w_row_ref[:, col + c:col + c + COL_CHUNK])
            out_ref[:, c:c + COL_CHUNK] = (t if scale is None else t * scale).astype(BF16)
        col += width
    row = 0
    for out_ref, height in ((rvt_ref, RET_V), (rgt_ref, RET_V), (avt_ref, ATT_W)):
        for r in range(0, height, COL_CHUNK):
            t = _dot_nt(w_t_ref[row + r:row + r + COL_CHUNK, :], xn)
            out_ref[r:r + COL_CHUNK, :] = t.astype(BF16)
        row += height


def _in_proj(x2d, g, w_row, w_t, cos, sin, seq):
    m = x2d.shape[0]
    steps_per_seq = seq // IN_ROW_TILE
    row = lambda width: pl.BlockSpec((IN_ROW_TILE, width), lambda i: (i, 0))
    col = lambda height: pl.BlockSpec((height, IN_ROW_TILE), lambda i: (0, i))
    pos = pl.BlockSpec((IN_ROW_TILE, RET_KEY_DIM), lambda i: (i % steps_per_seq, 0))
    row_widths = (RET_QK, RET_QK, ATT_W, ATT_W, 2 * D_MODEL)
    col_heights = (RET_V, RET_V, ATT_W)
    return pl.pallas_call(
        _in_proj_kernel,
        grid=(m // IN_ROW_TILE,),
        in_specs=[row(D_MODEL), _const_spec(g.shape), _const_spec(w_row.shape), _const_spec(w_t.shape),
                  pos, pos],
        out_specs=[row(w) for w in row_widths] + [col(h) for h in col_heights],
        out_shape=([jax.ShapeDtypeStruct((m, w), BF16) for w in row_widths]
                   + [jax.ShapeDtypeStruct((h, m), BF16) for h in col_heights]),
        compiler_params=pltpu.CompilerParams(dimension_semantics=("arbitrary",),
                                             vmem_limit_bytes=VMEM_LIMIT_BYTES),
        name="in_proj",
    )(x2d, g, w_row, w_t, cos, sin)


def _mixer_kernel(rq_ref, rk_ref, rvt_ref, rgt_ref, aq_ref, ak_ref, avt_ref,
                  ebias_ref, cap_ref, dect_ref, qdec_ref, kdec_ref, cdec_ref,
                  rogt_ref, aot_ref, kpad_ref, vtpad_ref, state_ref, biast_ref):
    b = pl.program_id(0)
    g = pl.program_id(1)
    n_pad_blocks = PAD_ROWS // Q_BLOCK

    @pl.when((b == 0) & (g == 0))
    def _():
        for h in range(ATT_HEADS):
            ext = jnp.broadcast_to(ebias_ref[h:h + 1, :], (BIAS_ROLL_ROWS, BIAS_EXT))
            for r in range(0, KEY_WINDOW, BIAS_ROLL_ROWS):
                rolled = pltpu.roll(ext, BIAS_EXT - KEY_WINDOW + 1 + r, axis=1, stride=1, stride_axis=0)
                biast_ref[h, r:r + BIAS_ROLL_ROWS, :] = rolled[:, :Q_BLOCK]

    @pl.when(g == 0)
    def _():
        kpad_ref[0:PAD_ROWS, :] = jnp.zeros((PAD_ROWS, ATT_W), BF16)
        kpad_ref[PAD_ROWS:, :] = ak_ref[...]
        for blk in range(n_pad_blocks):
            vtpad_ref[blk] = jnp.zeros((ATT_W, Q_BLOCK), BF16)
        for blk in range(avt_ref.shape[1] // Q_BLOCK):
            vtpad_ref[n_pad_blocks + blk] = avt_ref[:, blk * Q_BLOCK:(blk + 1) * Q_BLOCK]
        state_ref[...] = jnp.zeros(state_ref.shape, F32)

    pair_w = 2 * ATT_HEAD_DIM
    low_lanes = lax.broadcasted_iota(jnp.int32, (HALF_Q, pair_w), 1) < ATT_HEAD_DIM
    first_chunk_q = lax.broadcasted_iota(jnp.int32, (CHUNK, HALF_Q), 1) < CHUNK
    no_prob = jnp.zeros((KEY_WINDOW - HALF_KEYS, HALF_Q), BF16)
    ones_rows = jnp.ones((DENOM_ROWS, KEY_WINDOW), BF16)
    half_rows = [slice(half * HALF_Q, half * HALF_Q + HALF_KEYS) for half in range(2)]

    def run_blocks(step_block0, cap_blocks):
        def attention_scores(blk, p):
            cols = slice(p * pair_w, (p + 1) * pair_w)
            start = pl.multiple_of((step_block0 + blk) * Q_BLOCK, Q_BLOCK)
            kwin = kpad_ref[pl.ds(start, KEY_WINDOW), cols]
            scores = []
            for half in range(2):
                qrows = slice(blk * Q_BLOCK + half * HALF_Q, blk * Q_BLOCK + (half + 1) * HALF_Q)
                qp = aq_ref[qrows, cols]
                zero = jnp.zeros_like(qp)
                q2 = jnp.concatenate([jnp.where(low_lanes, qp, zero), jnp.where(low_lanes, zero, qp)], axis=0)
                scores.append(_dot_nt(kwin[half_rows[half], :], q2))
            return scores

        def attention_finish(blk, p, scores):
            vwin_t = jnp.concatenate([vtpad_ref[step_block0 + blk + i] for i in range(KEY_WINDOW // Q_BLOCK)],
                                     axis=1)
            out_lanes = slice(blk * Q_BLOCK, (blk + 1) * Q_BLOCK)
            for e in range(2):
                h = 2 * p + e
                probs = []
                for half in range(2):
                    lanes = slice(half * HALF_Q, (half + 1) * HALF_Q)
                    s = scores[half][:, e * HALF_Q:(e + 1) * HALF_Q] + biast_ref[h, half_rows[half], lanes]
                    if cap_blocks[blk] is None:
                        s = jnp.concatenate([jnp.where(first_chunk_q, s[:CHUNK], NEG_INF),
                                             s[CHUNK:HALF_KEYS - CHUNK],
                                             jnp.where(first_chunk_q, NEG_INF, s[HALF_KEYS - CHUNK:])], axis=0)
                    else:
                        s = jnp.minimum(s, cap_ref[cap_blocks[blk], half_rows[half], lanes])
                    probs.append(jnp.exp2(s - jnp.max(s, axis=0, keepdims=True)).astype(BF16))
                prob_t = jnp.concatenate([jnp.concatenate([probs[0], no_prob], axis=0),
                                          jnp.concatenate([no_prob, probs[1]], axis=0)], axis=1)
                head_rows = slice(h * ATT_HEAD_DIM, (h + 1) * ATT_HEAD_DIM)
                out_t = _dot(jnp.concatenate([vwin_t[head_rows, :], ones_rows], axis=0), prob_t)
                denom = out_t[ATT_HEAD_DIM:ATT_HEAD_DIM + 1, :]
                aot_ref[head_rows, out_lanes] = (out_t[:ATT_HEAD_DIM, :] * (1.0 / denom)).astype(BF16)

        def retention_dots(blk, h):
            kcols = slice(h * RET_KEY_DIM, (h + 1) * RET_KEY_DIM)
            pos = slice(blk * Q_BLOCK, (blk + 1) * Q_BLOCK)
            q = rq_ref[pos, kcols]
            k = rk_ref[pos, kcols]
            vt = rvt_ref[h * RET_VAL_DIM:(h + 1) * RET_VAL_DIM, pos]
            state_t = state_ref[h]
            qd = (q.astype(F32) * qdec_ref[:, kcols]).astype(BF16)
            kd = (k.astype(F32) * kdec_ref[:, kcols]).astype(BF16)
            scores_t = _dot_nt(k, q)
            carried = _dot_nt(state_t.astype(BF16), qd)
            state_ref[h] = state_t * cdec_ref[h] + _dot(vt, kd)
            return scores_t, carried

        def retention_finish(blk, h, scores_t, carried):
            vrows = slice(h * RET_VAL_DIM, (h + 1) * RET_VAL_DIM)
            pos = slice(blk * Q_BLOCK, (blk + 1) * Q_BLOCK)
            out_t = _dot(rvt_ref[vrows, pos], (scores_t * dect_ref[h]).astype(BF16)) + carried
            mu = jnp.mean(out_t, axis=0, keepdims=True)
            dev = out_t - mu
            var = jnp.mean(dev * dev, axis=0, keepdims=True)
            gate = rgt_ref[vrows, pos].astype(F32)
            rogt_ref[vrows, pos] = (_silu(gate) * (dev * lax.rsqrt(var + EPS))).astype(BF16)

        assert ATT_HEADS // 2 == RET_HEADS
        items = [(blk, i) for blk in range(len(cap_blocks)) for i in range(RET_HEADS)]
        lead = lambda blk, i: (attention_scores(blk, i), retention_dots(blk, i))
        nxt = lead(*items[0])
        for idx, (blk, i) in enumerate(items):
            now = nxt
            if idx + 1 < len(items):
                nxt = lead(*items[idx + 1])
            attention_finish(blk, i, now[0])
            retention_finish(blk, i, *now[1])

    assert n_pad_blocks == STEP_BLOCKS

    @pl.when(g == 0)
    def _():
        run_blocks(0, list(range(STEP_BLOCKS)))

    @pl.when(g > 0)
    def _():
        run_blocks(g * STEP_BLOCKS, [None] * STEP_BLOCKS)


def _mixers(rq, rk, rvt, rgt, aq, ak, avt, ebias, cap, dect, qdec, kdec, cdec, batch, seq):
    m = rq.shape[0]
    q_step = STEP_BLOCKS * Q_BLOCK
    steps_per_seq = seq // q_step
    qrow = lambda width: pl.BlockSpec((q_step, width), lambda b, g: (b * steps_per_seq + g, 0))
    qcol = lambda height: pl.BlockSpec((height, q_step), lambda b, g: (0, b * steps_per_seq + g))
    return pl.pallas_call(
        _mixer_kernel,
        grid=(batch, steps_per_seq),
        in_specs=[qrow(RET_QK), qrow(RET_QK), qcol(RET_V), qcol(RET_V), qrow(ATT_W),
                  pl.BlockSpec((seq, ATT_W), lambda b, g: (b, 0)),
                  pl.BlockSpec((ATT_W, seq), lambda b, g: (0, b)),
                  _const_spec(ebias.shape), _const_spec(cap.shape), _const_spec(dect.shape),
                  _const_spec(qdec.shape), _const_spec(kdec.shape), _const_spec(cdec.shape)],
        out_specs=[qcol(RET_V), qcol(ATT_W)],
        out_shape=[jax.ShapeDtypeStruct((RET_V, m), BF16), jax.ShapeDtypeStruct((ATT_W, m), BF16)],
        scratch_shapes=[pltpu.VMEM((PAD_ROWS + seq, ATT_W), BF16),
                        pltpu.VMEM(((PAD_ROWS + seq) // Q_BLOCK, ATT_W, Q_BLOCK), BF16),
                        pltpu.VMEM((RET_HEADS, RET_VAL_DIM, RET_KEY_DIM), F32),
                        pltpu.VMEM((ATT_HEADS, KEY_WINDOW, Q_BLOCK), F32)],
        compiler_params=pltpu.CompilerParams(dimension_semantics=("arbitrary", "arbitrary"),
                                             vmem_limit_bytes=VMEM_LIMIT_BYTES),
        name="mixers",
    )(rq, rk, rvt, rgt, aq, ak, avt, ebias, cap, dect, qdec, kdec, cdec)


def _merge_ffn_kernel(x_ref, rogt_ref, aot_ref, gl_ref, bg_ref, wro_ref, wao_ref, wout_ref,
                      gf_ref, wg_ref, wu_ref, wd_ref, gn_ref, o_ref):
    y_ret = _dot_tn(rogt_ref[...], wro_ref[...])
    y_att = _dot_tn(aot_ref[...], wao_ref[...])
    t_ret = jnp.tanh(gl_ref[:, :D_MODEL].astype(F32) + bg_ref[:, :D_MODEL])
    t_att = jnp.tanh(gl_ref[:, D_MODEL:].astype(F32) + bg_ref[:, D_MODEL:])
    mixed = ((y_ret + t_ret * y_ret) + (y_att + t_att * y_att)).astype(BF16)
    h = x_ref[...] + _dot(mixed, wout_ref[...])
    hn = ((h * _rms_scale(h)) * gf_ref[...]).astype(BF16)
    d_ff = wg_ref.shape[1]
    ffn = None
    for c in range(0, d_ff, FFN_CHUNK):
        cols = slice(c, min(c + FFN_CHUNK, d_ff))
        gate = _dot(hn, wg_ref[:, cols])
        up = _dot(hn, wu_ref[:, cols])
        part = _dot((_silu(gate) * up).astype(BF16), wd_ref[cols, :])
        ffn = part if ffn is None else ffn + part
    h2 = h + ffn
    o_ref[...] = (h2 * _rms_scale(h2)) * gn_ref[...]


def _merge_ffn(x2d, rogt, aot, gl, b_gate, w_ret_out, w_att_out, w_out, norm_ffn, w_gate, w_up, w_down,
               norm_final):
    m = x2d.shape[0]
    assert w_gate.shape[1] % V7X_MXU_DIM == 0
    row = lambda width: pl.BlockSpec((ROW_TILE, width), lambda i: (i, 0))
    col = lambda height: pl.BlockSpec((height, ROW_TILE), lambda i: (0, i))
    consts = (b_gate, w_ret_out, w_att_out, w_out, norm_ffn, w_gate, w_up, w_down, norm_final)
    return pl.pallas_call(
        _merge_ffn_kernel,
        grid=(m // ROW_TILE,),
        in_specs=[row(D_MODEL), col(RET_V), col(ATT_W), row(2 * D_MODEL)] + [_const_spec(c.shape) for c in consts],
        out_specs=row(D_MODEL),
        out_shape=jax.ShapeDtypeStruct((m, D_MODEL), F32),
        compiler_params=pltpu.CompilerParams(dimension_semantics=("arbitrary",),
                                             vmem_limit_bytes=VMEM_LIMIT_BYTES),
        name="merge_ffn",
    )(x2d, rogt, aot, gl, *consts)


def _rotary_tables(seq):
    d = RET_KEY_DIM
    freqs = ROPE_BASE ** (-np.arange(0, d, 2, dtype=np.float64) / d)
    ang = np.arange(seq, dtype=np.float64)[:, None] * freqs[None, :]
    cos = np.cos(ang)
    sin = np.sin(ang)
    return (jnp.asarray(np.concatenate([cos, cos], axis=-1), F32),
            jnp.asarray(np.concatenate([-sin, sin], axis=-1), F32))


def _retention_tables():
    log_g = np.log(1.0 - 2.0 ** (-5.0 - np.arange(RET_HEADS, dtype=np.float64)))
    p = np.arange(Q_BLOCK, dtype=np.float64)
    dist = p[None, :] - p[:, None]
    chunk = np.arange(Q_BLOCK) // CHUNK
    same = chunk[:, None] == chunk[None, :]
    earlier = chunk[:, None] < chunk[None, :]
    decay = np.exp(log_g[:, None, None] * np.where(same, np.abs(dist), dist)[None])
    dect = np.where((same | earlier)[None], decay, 0.0)
    q_dec = np.exp(log_g[None, :] * (p[:, None] + 1.0))
    k_dec = np.exp(log_g[None, :] * (Q_BLOCK - 1.0 - p[:, None]))
    qdec = np.repeat(q_dec, RET_KEY_DIM, axis=1)
    kdec = np.repeat(k_dec, RET_KEY_DIM, axis=1)
    cdec = np.broadcast_to(np.exp(log_g * Q_BLOCK)[:, None, None], (RET_HEADS, 1, RET_KEY_DIM))
    return tuple(jnp.asarray(t, F32) for t in (dect, qdec, kdec, cdec))


def _extended_bias(rel_bias):
    left = Q_BLOCK - CHUNK
    return jnp.pad(rel_bias.astype(F32) * LOG2E, ((0, 0), (left, BIAS_EXT - left - N_REL)), mode="edge")


def _visibility_caps():
    j = np.arange(KEY_WINDOW)[:, None]
    n = np.arange(Q_BLOCK)[None, :]
    in_band = (j // CHUNK >= n // CHUNK) & (j // CHUNK <= n // CHUNK + BAND_CHUNKS)
    caps = [np.where(in_band & (g * Q_BLOCK - PAD_ROWS + j >= 0), F32_MAX, NEG_INF)
            for g in range(PAD_ROWS // Q_BLOCK)]
    return jnp.asarray(np.stack(caps), F32)


def _split_w_in(w_in):
    sizes = (RET_QK, RET_QK, RET_V, RET_V, ATT_W, ATT_W, ATT_W, 2 * D_MODEL)
    rq, rk, rv, rg, aq, ak, av, gl = jnp.split(w_in, np.cumsum(sizes)[:-1].tolist(), axis=1)
    w_row = jnp.concatenate([rq, rk, aq, ak, gl], axis=1).astype(BF16)
    w_t = jnp.concatenate([rv, rg, av], axis=1).T.astype(BF16)
    return w_row, w_t


def kernel(x, norm_mix, w_in, b_gate, rel_bias, w_ret_out, w_att_out, w_out, norm_ffn,
           w_ffn_gate, w_ffn_up, w_ffn_down, norm_final):
    batch, seq, d_model = x.shape
    assert w_in.shape[0] == 1, "single-layer block"
    assert d_model == D_MODEL and seq % ROW_TILE == 0 and seq % IN_ROW_TILE == 0
    assert seq % (STEP_BLOCKS * Q_BLOCK) == 0 and PAD_ROWS == STEP_BLOCKS * Q_BLOCK
    cos, sin = _rotary_tables(seq)
    dect, qdec, kdec, cdec = _retention_tables()
    w_row, w_t = _split_w_in(w_in[0])
    x2d = x.reshape(batch * seq, d_model)
    rq, rk, aq, ak, gl, rvt, rgt, avt = _in_proj(x2d, norm_mix, w_row, w_t, cos, sin, seq)
    rogt, aot = _mixers(rq, rk, rvt, rgt, aq, ak, avt, _extended_bias(rel_bias[0]), _visibility_caps(),
                        dect, qdec, kdec, cdec, batch, seq)
    out = _merge_ffn(x2d, rogt, aot, gl, 0.5 * b_gate, (0.5 * w_ret_out[0]).astype(BF16),
                     (0.5 * w_att_out[0]).astype(BF16), w_out[0].astype(BF16), norm_ffn,
                     w_ffn_gate[0].astype(BF16), w_ffn_up[0].astype(BF16), w_ffn_down[0].astype(BF16),
                     norm_final[None, :])
    return out.reshape(batch, seq, d_model)
```

```python
import jax
import jax.numpy as jnp
import numpy as np
from jax import lax
from jax.experimental import pallas as pl
from jax.experimental.pallas import tpu as pltpu

D_MODEL = 1024
CHUNK = 64
RET_HEADS = 4
RET_KEY_DIM = 128
RET_VAL_DIM = 256
RET_QK = RET_HEADS * RET_KEY_DIM
RET_V = RET_HEADS * RET_VAL_DIM
ATT_HEADS = 8
ATT_HEAD_DIM = 64
ATT_W = ATT_HEADS * ATT_HEAD_DIM
BAND_CHUNKS = 8
MAX_REL = 256
N_REL = CHUNK + MAX_REL
ROPE_BASE = 10000.0
EPS = 1e-6
NEG_INF = -1e30
F32_MAX = float(np.finfo(np.float32).max)

V7X_LANES = 128
V7X_MXU_DIM = 256
VMEM_LIMIT_BYTES = 60 * 1024 * 1024

IN_ROW_TILE = 1024
ROW_TILE = 512
COL_CHUNK = 512
FFN_CHUNK = 6 * V7X_MXU_DIM
STAGE_ROWS = 256
Q_BLOCK = 4 * CHUNK
STEP_BLOCKS = 2
PAD_ROWS = BAND_CHUNKS * CHUNK
KEY_WINDOW = Q_BLOCK + PAD_ROWS
HALF_Q = Q_BLOCK // 2
HALF_KEYS = HALF_Q + PAD_ROWS
BIAS_EXT = 1024
BIAS_ROLL_ROWS = 128
DENOM_ROWS = 16
LOG2E = 1.4426950408889634

assert HALF_Q == V7X_LANES and KEY_WINDOW - HALF_KEYS == HALF_Q
assert BIAS_EXT >= KEY_WINDOW + Q_BLOCK - 1

BF16 = jnp.bfloat16
F32 = jnp.float32


def _const_spec(shape):
    zeros = (0,) * len(shape)
    return pl.BlockSpec(shape, lambda *_: zeros, pipeline_mode=pl.Buffered(1))


def _dot(a, b):
    return jnp.dot(a, b, preferred_element_type=F32)


def _dot_nt(a, b):
    return lax.dot_general(a, b, (((1,), (1,)), ((), ())), preferred_element_type=F32)


def _dot_tn(a, b):
    return lax.dot_general(a, b, (((0,), (0,)), ((), ())), preferred_element_type=F32)


def _silu(x):
    u = 0.5 * x
    return u + u * jnp.tanh(u)


def _rms_scale(x):
    return lax.rsqrt(jnp.mean(x * x, axis=-1, keepdims=True) + EPS)


def _in_proj_kernel(x_ref, g_ref, w_row_ref, w_t_ref, cos_ref, sin_ref,
                    rq_ref, rk_ref, aq_ref, ak_ref, gl_ref, rvt_ref, rgt_ref, avt_ref):
    x = x_ref[...]
    xn = ((x * _rms_scale(x)) * g_ref[...]).astype(BF16)
    cos = cos_ref[...]
    sin = sin_ref[...]

    def rotary_store(out_ref, col0, scale):
        t = _dot(xn, w_row_ref[:, col0:col0 + RET_QK])
        for h in range(RET_HEADS):
            cols = slice(h * RET_KEY_DIM, (h + 1) * RET_KEY_DIM)
            th = t[:, cols]
            rot = th * cos + pltpu.roll(th, RET_KEY_DIM // 2, axis=1) * sin
            out_ref[:, cols] = (rot if scale is None else rot * scale).astype(BF16)

    rotary_store(rq_ref, 0, None)
    rotary_store(rk_ref, RET_QK, RET_KEY_DIM ** -0.5)
    col = 2 * RET_QK
    for out_ref, width, scale in ((aq_ref, ATT_W, ATT_HEAD_DIM ** -0.5 * LOG2E), (ak_ref, ATT_W, None),
                                  (gl_ref, 2 * D_MODEL, 0.5)):
        for c in range(0, width, COL_CHUNK):
            t = _dot(xn, w_row_ref[:, col + c:col + c + COL_CHUNK])
            out_ref[:, c:c + COL_CHUNK] = (t if scale is None else t * scale).astype(BF16)
        col += width
    row = 0
    for out_ref, height in ((rvt_ref, RET_V), (rgt_ref, RET_V), (avt_ref, ATT_W)):
        for r in range(0, height, COL_CHUNK):
            t = _dot_nt(w_t_ref[row + r:row + r + COL_CHUNK, :], xn)
            out_ref[r:r + COL_CHUNK, :] = t.astype(BF16)
        row += height


def _in_proj(x2d, g, w_row, w_t, cos, sin, seq):
    m = x2d.shape[0]
    steps_per_seq = seq // IN_ROW_TILE
    row = lambda width: pl.BlockSpec((IN_ROW_TILE, width), lambda i: (i, 0))
    col = lambda height: pl.BlockSpec((height, IN_ROW_TILE), lambda i: (0, i))
    pos = pl.BlockSpec((IN_ROW_TILE, RET_KEY_DIM), lambda i: (i % steps_per_seq, 0))
    row_widths = (RET_QK, RET_QK, ATT_W, ATT_W, 2 * D_MODEL)
    col_heights = (RET_V, RET_V, ATT_W)
    return pl.pallas_call(
        _in_proj_kernel,
        grid=(m // IN_ROW_TILE,),
        in_specs=[row(D_MODEL), _const_spec(g.shape), _const_spec(w_row.shape), _const_spec(w_t.shape),
                  pos, pos],
        out_specs=[row(w) for w in row_widths] + [col(h) for h in col_heights],
        out_shape=([jax.ShapeDtypeStruct((m, w), BF16) for w in row_widths]
                   + [jax.ShapeDtypeStruct((h, m), BF16) for h in col_heights]),
        compiler_params=pltpu.CompilerParams(dimension_semantics=("arbitrary",),
                                             vmem_limit_bytes=VMEM_LIMIT_BYTES),
        name="in_proj",
    )(x2d, g, w_row, w_t, cos, sin)


def _mixer_kernel(rq_ref, rk_ref, rvt_ref, rgt_ref, aq_ref, ak_ref, avt_ref,
                  ebias_ref, cap_ref, dect_ref, qdec_ref, kdec_ref, cdec_ref,
                  rogt_ref, aot_ref, kpad_ref, vtpad_ref, state_ref, biast_ref):
    b = pl.program_id(0)
    g = pl.program_id(1)
    n_pad_blocks = PAD_ROWS // Q_BLOCK

    @pl.when((b == 0) & (g == 0))
    def _():
        for h in range(ATT_HEADS):
            ext = jnp.broadcast_to(ebias_ref[h:h + 1, :], (BIAS_ROLL_ROWS, BIAS_EXT))
            for r in range(0, KEY_WINDOW, BIAS_ROLL_ROWS):
                rolled = pltpu.roll(ext, BIAS_EXT - KEY_WINDOW + 1 + r, axis=1, stride=1, stride_axis=0)
                biast_ref[h, r:r + BIAS_ROLL_ROWS, :] = rolled[:, :Q_BLOCK]

    @pl.when(g == 0)
    def _():
        kpad_ref[0:PAD_ROWS, :] = jnp.zeros((PAD_ROWS, ATT_W), BF16)
        kpad_ref[PAD_ROWS:, :] = ak_ref[...]
        for blk in range(n_pad_blocks):
            vtpad_ref[blk] = jnp.zeros((ATT_W, Q_BLOCK), BF16)
        for blk in range(avt_ref.shape[1] // Q_BLOCK):
            vtpad_ref[n_pad_blocks + blk] = avt_ref[:, blk * Q_BLOCK:(blk + 1) * Q_BLOCK]
        state_ref[...] = jnp.zeros(state_ref.shape, F32)

    pair_w = 2 * ATT_HEAD_DIM
    low_lanes = lax.broadcasted_iota(jnp.int32, (HALF_Q, pair_w), 1) < ATT_HEAD_DIM
    first_chunk_q = lax.broadcasted_iota(jnp.int32, (CHUNK, HALF_Q), 1) < CHUNK
    no_prob = jnp.zeros((KEY_WINDOW - HALF_KEYS, HALF_Q), BF16)
    ones_rows = jnp.ones((DENOM_ROWS, KEY_WINDOW), BF16)
    half_rows = [slice(half * HALF_Q, half * HALF_Q + HALF_KEYS) for half in range(2)]

    def run_blocks(step_block0, cap_blocks):
        def attention_scores(blk, p):
            cols = slice(p * pair_w, (p + 1) * pair_w)
            start = pl.multiple_of((step_block0 + blk) * Q_BLOCK, Q_BLOCK)
            kwin = kpad_ref[pl.ds(start, KEY_WINDOW), cols]
            scores = []
            for half in range(2):
                qrows = slice(blk * Q_BLOCK + half * HALF_Q, blk * Q_BLOCK + (half + 1) * HALF_Q)
                qp = aq_ref[qrows, cols]
                zero = jnp.zeros_like(qp)
                q2 = jnp.concatenate([jnp.where(low_lanes, qp, zero), jnp.where(low_lanes, zero, qp)], axis=0)
                scores.append(_dot_nt(kwin[half_rows[half], :], q2))
            return scores

        def attention_finish(blk, p, scores):
            vwin_t = jnp.concatenate([vtpad_ref[step_block0 + blk + i] for i in range(KEY_WINDOW // Q_BLOCK)],
                                     axis=1)
            out_lanes = slice(blk * Q_BLOCK, (blk + 1) * Q_BLOCK)
            for e in range(2):
                h = 2 * p + e
                probs = []
                for half in range(2):
                    lanes = slice(half * HALF_Q, (half + 1) * HALF_Q)
                    s = scores[half][:, e * HALF_Q:(e + 1) * HALF_Q] + biast_ref[h, half_rows[half], lanes]
                    if cap_blocks[blk] is None:
                        s = jnp.concatenate([jnp.where(first_chunk_q, s[:CHUNK], NEG_INF),
                                             s[CHUNK:HALF_KEYS - CHUNK],
                                             jnp.where(first_chunk_q, NEG_INF, s[HALF_KEYS - CHUNK:])], axis=0)
                    else:
                        s = jnp.minimum(s, cap_ref[cap_blocks[blk], half_rows[half], lanes])
                    probs.append(jnp.exp2(s - jnp.max(s, axis=0, keepdims=True)).astype(BF16))
                prob_t = jnp.concatenate([jnp.concatenate([probs[0], no_prob], axis=0),
                                          jnp.concatenate([no_prob, probs[1]], axis=0)], axis=1)
                head_rows = slice(h * ATT_HEAD_DIM, (h + 1) * ATT_HEAD_DIM)
                out_t = _dot(jnp.concatenate([vwin_t[head_rows, :], ones_rows], axis=0), prob_t)
                denom = out_t[ATT_HEAD_DIM:ATT_HEAD_DIM + 1, :]
                aot_ref[head_rows, out_lanes] = (out_t[:ATT_HEAD_DIM, :] * (1.0 / denom)).astype(BF16)

        def retention_dots(blk, h):
            kcols = slice(h * RET_KEY_DIM, (h + 1) * RET_KEY_DIM)
            pos = slice(blk * Q_BLOCK, (blk + 1) * Q_BLOCK)
            q = rq_ref[pos, kcols]
            k = rk_ref[pos, kcols]
            vt = rvt_ref[h * RET_VAL_DIM:(h + 1) * RET_VAL_DIM, pos]
            state_t = state_ref[h]
            qd = (q.astype(F32) * qdec_ref[:, kcols]).astype(BF16)
            kd = (k.astype(F32) * kdec_ref[:, kcols]).astype(BF16)
            scores_t = _dot_nt(k, q)
            carried = _dot_nt(state_t.astype(BF16), qd)
            state_ref[h] = state_t * cdec_ref[h] + _dot(vt, kd)
            return scores_t, carried

        def retention_finish(blk, h, scores_t, carried):
            vrows = slice(h * RET_VAL_DIM, (h + 1) * RET_VAL_DIM)
            pos = slice(blk * Q_BLOCK, (blk + 1) * Q_BLOCK)
            out_t = _dot(rvt_ref[vrows, pos], (scores_t * dect_ref[h]).astype(BF16)) + carried
            mu = jnp.mean(out_t, axis=0, keepdims=True)
            dev = out_t - mu
            var = jnp.mean(dev * dev, axis=0, keepdims=True)
            gate = rgt_ref[vrows, pos].astype(F32)
            rogt_ref[vrows, pos] = (_silu(gate) * (dev * lax.rsqrt(var + EPS))).astype(BF16)

        assert ATT_HEADS // 2 == RET_HEADS
        items = [(blk, i) for blk in range(len(cap_blocks)) for i in range(RET_HEADS)]
        lead = lambda blk, i: (attention_scores(blk, i), retention_dots(blk, i))
        nxt = lead(*items[0])
        for idx, (blk, i) in enumerate(items):
            now = nxt
            if idx + 1 < len(items):
                nxt = lead(*items[idx + 1])
            attention_finish(blk, i, now[0])
            retention_finish(blk, i, *now[1])

    assert n_pad_blocks == STEP_BLOCKS

    @pl.when(g == 0)
    def _():
        run_blocks(0, list(range(STEP_BLOCKS)))

    @pl.when(g > 0)
    def _():
        run_blocks(g * STEP_BLOCKS, [None] * STEP_BLOCKS)


def _mixers(rq, rk, rvt, rgt, aq, ak, avt, ebias, cap, dect, qdec, kdec, cdec, batch, seq):
    m = rq.shape[0]
    q_step = STEP_BLOCKS * Q_BLOCK
    steps_per_seq = seq // q_step
    qrow = lambda width: pl.BlockSpec((q_step, width), lambda b, g: (b * steps_per_seq + g, 0))
    qcol = lambda height: pl.BlockSpec((height, q_step), lambda b, g: (0, b * steps_per_seq + g))
    return pl.pallas_call(
        _mixer_kernel,
        grid=(batch, steps_per_seq),
        in_specs=[qrow(RET_QK), qrow(RET_QK), qcol(RET_V), qcol(RET_V), qrow(ATT_W),
                  pl.BlockSpec((seq, ATT_W), lambda b, g: (b, 0)),
                  pl.BlockSpec((ATT_W, seq), lambda b, g: (0, b)),
                  _const_spec(ebias.shape), _const_spec(cap.shape), _const_spec(dect.shape),
                  _const_spec(qdec.shape), _const_spec(kdec.shape), _const_spec(cdec.shape)],
        out_specs=[qcol(RET_V), qcol(ATT_W)],
        out_shape=[jax.ShapeDtypeStruct((RET_V, m), BF16), jax.ShapeDtypeStruct((ATT_W, m), BF16)],
        scratch_shapes=[pltpu.VMEM((PAD_ROWS + seq, ATT_W), BF16),
                        pltpu.VMEM(((PAD_ROWS + seq) // Q_BLOCK, ATT_W, Q_BLOCK), BF16),
                        pltpu.VMEM((RET_HEADS, RET_VAL_DIM, RET_KEY_DIM), F32),
                        pltpu.VMEM((ATT_HEADS, KEY_WINDOW, Q_BLOCK), F32)],
        compiler_params=pltpu.CompilerParams(dimension_semantics=("arbitrary", "arbitrary"),
                                             vmem_limit_bytes=VMEM_LIMIT_BYTES),
        name="mixers",
    )(rq, rk, rvt, rgt, aq, ak, avt, ebias, cap, dect, qdec, kdec, cdec)


def _stage_weights(jobs, stage_ref, sem_ref):
    pieces = [(src, dst, r, scale) for src, dst, scale in jobs for r in range(0, dst.shape[0], STAGE_ROWS)]

    def copy(k):
        src, dst, r, _ = pieces[k]
        return pltpu.make_async_copy(src.at[pl.ds(r, STAGE_ROWS), :],
                                     stage_ref.at[k % 2, :, pl.ds(0, dst.shape[1])], sem_ref.at[k % 2])

    copy(0).start()
    for k, (_, dst, r, scale) in enumerate(pieces):
        if k + 1 < len(pieces):
            copy(k + 1).start()
        copy(k).wait()
        piece = stage_ref[k % 2, :, 0:dst.shape[1]]
        dst[r:r + STAGE_ROWS, :] = (piece if scale is None else piece * scale).astype(BF16)


def _merge_ffn_kernel(x_ref, rogt_ref, aot_ref, gl_ref, bg_ref, gf_ref, gn_ref,
                      wro_hbm, wao_hbm, wout_hbm, wg_hbm, wu_hbm, wd_hbm, o_ref,
                      wro_ref, wao_ref, wout_ref, wg_ref, wu_ref, wd_ref, stage_ref, sem_ref):
    @pl.when(pl.program_id(0) == 0)
    def _():
        _stage_weights([(wro_hbm, wro_ref, 0.5), (wao_hbm, wao_ref, 0.5), (wout_hbm, wout_ref, None),
                        (wg_hbm, wg_ref, None), (wu_hbm, wu_ref, None), (wd_hbm, wd_ref, None)],
                       stage_ref, sem_ref)

    y_ret = _dot_tn(rogt_ref[...], wro_ref[...])
    y_att = _dot_tn(aot_ref[...], wao_ref[...])
    t_ret = jnp.tanh(gl_ref[:, :D_MODEL].astype(F32) + bg_ref[:, :D_MODEL])
    t_att = jnp.tanh(gl_ref[:, D_MODEL:].astype(F32) + bg_ref[:, D_MODEL:])
    mixed = ((y_ret + t_ret * y_ret) + (y_att + t_att * y_att)).astype(BF16)
    h = x_ref[...] + _dot(mixed, wout_ref[...])
    hn = ((h * _rms_scale(h)) * gf_ref[...]).astype(BF16)
    d_ff = wg_ref.shape[1]
    ffn = None
    for c in range(0, d_ff, FFN_CHUNK):
        cols = slice(c, min(c + FFN_CHUNK, d_ff))
        gate = _dot(hn, wg_ref[:, cols])
        up = _dot(hn, wu_ref[:, cols])
        part = _dot((_silu(gate) * up).astype(BF16), wd_ref[cols, :])
        ffn = part if ffn is None else ffn + part
    h2 = h + ffn
    o_ref[...] = (h2 * _rms_scale(h2)) * gn_ref[...]


def _merge_ffn(x2d, rogt, aot, gl, b_gate, w_ret_out, w_att_out, w_out, norm_ffn, w_gate, w_up, w_down,
               norm_final):
    m = x2d.shape[0]
    weights = (w_ret_out, w_att_out, w_out, w_gate, w_up, w_down)
    assert w_gate.shape[1] % V7X_MXU_DIM == 0 and all(w.shape[0] % STAGE_ROWS == 0 for w in weights)
    row = lambda width: pl.BlockSpec((ROW_TILE, width), lambda i: (i, 0))
    col = lambda height: pl.BlockSpec((height, ROW_TILE), lambda i: (0, i))
    small = (b_gate, norm_ffn, norm_final)
    return pl.pallas_call(
        _merge_ffn_kernel,
        grid=(m // ROW_TILE,),
        in_specs=([row(D_MODEL), col(RET_V), col(ATT_W), row(2 * D_MODEL)] + [_const_spec(c.shape) for c in small]
                  + [pl.BlockSpec(memory_space=pl.ANY)] * len(weights)),
        out_specs=row(D_MODEL),
        out_shape=jax.ShapeDtypeStruct((m, D_MODEL), F32),
        scratch_shapes=([pltpu.VMEM(w.shape, BF16) for w in weights]
                        + [pltpu.VMEM((2, STAGE_ROWS, max(w.shape[1] for w in weights)), F32),
                           pltpu.SemaphoreType.DMA((2,))]),
        compiler_params=pltpu.CompilerParams(dimension_semantics=("arbitrary",),
                                             vmem_limit_bytes=VMEM_LIMIT_BYTES),
        name="merge_ffn",
    )(x2d, rogt, aot, gl, *small, *weights)


def _rotary_tables(seq):
    d = RET_KEY_DIM
    freqs = ROPE_BASE ** (-np.arange(0, d, 2, dtype=np.float64) / d)
    ang = np.arange(seq, dtype=np.float64)[:, None] * freqs[None, :]
    cos = np.cos(ang)
    sin = np.sin(ang)
    return (jnp.asarray(np.concatenate([cos, cos], axis=-1), F32),
            jnp.asarray(np.concatenate([-sin, sin], axis=-1), F32))


def _retention_tables():
    log_g = np.log(1.0 - 2.0 ** (-5.0 - np.arange(RET_HEADS, dtype=np.float64)))
    p = np.arange(Q_BLOCK, dtype=np.float64)
    dist = p[None, :] - p[:, None]
    chunk = np.arange(Q_BLOCK) // CHUNK
    same = chunk[:, None] == chunk[None, :]
    earlier = chunk[:, None] < chunk[None, :]
    decay = np.exp(log_g[:, None, None] * np.where(same, np.abs(dist), dist)[None])
    dect = np.where((same | earlier)[None], decay, 0.0)
    q_dec = np.exp(log_g[None, :] * (p[:, None] + 1.0))
    k_dec = np.exp(log_g[None, :] * (Q_BLOCK - 1.0 - p[:, None]))
    qdec = np.repeat(q_dec, RET_KEY_DIM, axis=1)
    kdec = np.repeat(k_dec, RET_KEY_DIM, axis=1)
    cdec = np.broadcast_to(np.exp(log_g * Q_BLOCK)[:, None, None], (RET_HEADS, 1, RET_KEY_DIM))
    return tuple(jnp.asarray(t, F32) for t in (dect, qdec, kdec, cdec))


def _extended_bias(rel_bias):
    left = Q_BLOCK - CHUNK
    return jnp.pad(rel_bias.astype(F32) * LOG2E, ((0, 0), (left, BIAS_EXT - left - N_REL)), mode="edge")


def _visibility_caps():
    j = np.arange(KEY_WINDOW)[:, None]
    n = np.arange(Q_BLOCK)[None, :]
    in_band = (j // CHUNK >= n // CHUNK) & (j // CHUNK <= n // CHUNK + BAND_CHUNKS)
    caps = [np.where(in_band & (g * Q_BLOCK - PAD_ROWS + j >= 0), F32_MAX, NEG_INF)
            for g in range(PAD_ROWS // Q_BLOCK)]
    return jnp.asarray(np.stack(caps), F32)


def _split_w_in(w_in):
    sizes = (RET_QK, RET_QK, RET_V, RET_V, ATT_W, ATT_W, ATT_W, 2 * D_MODEL)
    rq, rk, rv, rg, aq, ak, av, gl = jnp.split(w_in, np.cumsum(sizes)[:-1].tolist(), axis=1)
    w_row = jnp.concatenate([rq, rk, aq, ak, gl], axis=1).astype(BF16)
    w_t = jnp.concatenate([rv, rg, av], axis=1).T.astype(BF16)
    return w_row, w_t


def kernel(x, norm_mix, w_in, b_gate, rel_bias, w_ret_out, w_att_out, w_out, norm_ffn,
           w_ffn_gate, w_ffn_up, w_ffn_down, norm_final):
    batch, seq, d_model = x.shape
    assert w_in.shape[0] == 1, "single-layer block"
    assert d_model == D_MODEL and seq % ROW_TILE == 0 and seq % IN_ROW_TILE == 0
    assert seq % (STEP_BLOCKS * Q_BLOCK) == 0 and PAD_ROWS == STEP_BLOCKS * Q_BLOCK
    cos, sin = _rotary_tables(seq)
    dect, qdec, kdec, cdec = _retention_tables()
    w_row, w_t = _split_w_in(w_in[0])
    x2d = x.reshape(batch * seq, d_model)
    rq, rk, aq, ak, gl, rvt, rgt, avt = _in_proj(x2d, norm_mix, w_row, w_t, cos, sin, seq)
    rogt, aot = _mixers(rq, rk, rvt, rgt, aq, ak, avt, _extended_bias(rel_bias[0]), _visibility_caps(),
                        dect, qdec, kdec, cdec, batch, seq)
    out = _merge_ffn(x2d, rogt, aot, gl, 0.5 * b_gate, w_ret_out[0], w_att_out[0], w_out[0], norm_ffn,
                     w_ffn_gate[0], w_ffn_up[0], w_ffn_down[0], norm_final[None, :])
    return out.reshape(batch, seq, d_model)
```

```python
import jax
import jax.numpy as jnp
import numpy as np
from jax import lax
from jax.experimental import pallas as pl
from jax.experimental.pallas import tpu as pltpu

D_MODEL = 1024
CHUNK = 64
RET_HEADS = 4
RET_KEY_DIM = 128
RET_VAL_DIM = 256
RET_QK = RET_HEADS * RET_KEY_DIM
RET_V = RET_HEADS * RET_VAL_DIM
ATT_HEADS = 8
ATT_HEAD_DIM = 64
ATT_W = ATT_HEADS * ATT_HEAD_DIM
BAND_CHUNKS = 8
MAX_REL = 256
N_REL = CHUNK + MAX_REL
ROPE_BASE = 10000.0
EPS = 1e-6
NEG_INF = -1e30
F32_MAX = float(np.finfo(np.float32).max)

V7X_LANES = 128
V7X_MXU_DIM = 256
VMEM_LIMIT_BYTES = 60 * 1024 * 1024

IN_ROW_TILE = 1024
ROW_TILE = 512
COL_CHUNK = 512
FFN_CHUNK = 6 * V7X_MXU_DIM
STAGE_ROWS = 256
IN_STAGE_ROWS = 128
IN_STAGE_COLS = 13 * V7X_LANES
Q_BLOCK = 4 * CHUNK
STEP_BLOCKS = 2
PAD_ROWS = BAND_CHUNKS * CHUNK
KEY_WINDOW = Q_BLOCK + PAD_ROWS
HALF_Q = Q_BLOCK // 2
HALF_KEYS = HALF_Q + PAD_ROWS
BIAS_EXT = 1024
BIAS_ROLL_ROWS = 128
DENOM_ROWS = 16
LOG2E = 1.4426950408889634

assert HALF_Q == V7X_LANES and KEY_WINDOW - HALF_KEYS == HALF_Q
assert BIAS_EXT >= KEY_WINDOW + Q_BLOCK - 1

BF16 = jnp.bfloat16
F32 = jnp.float32


def _const_spec(shape):
    zeros = (0,) * len(shape)
    return pl.BlockSpec(shape, lambda *_: zeros, pipeline_mode=pl.Buffered(1))


def _dot(a, b):
    return jnp.dot(a, b, preferred_element_type=F32)


def _dot_nt(a, b):
    return lax.dot_general(a, b, (((1,), (1,)), ((), ())), preferred_element_type=F32)


def _dot_tn(a, b):
    return lax.dot_general(a, b, (((0,), (0,)), ((), ())), preferred_element_type=F32)


def _silu(x):
    u = 0.5 * x
    return u + u * jnp.tanh(u)


def _rms_scale(x):
    return lax.rsqrt(jnp.mean(x * x, axis=-1, keepdims=True) + EPS)


IN_SEGMENTS = ((RET_QK, False), (RET_QK, False), (RET_V, True), (RET_V, True),
               (ATT_W, False), (ATT_W, False), (ATT_W, True), (2 * D_MODEL, False))


def _in_layout():
    layout, col, offsets = [], 0, {False: 0, True: 0}
    for width, transposed in IN_SEGMENTS:
        layout.append((col, width, transposed, offsets[transposed]))
        col += width
        offsets[transposed] += width
    return layout, offsets[False], offsets[True]


def _stage_w_in(w_hbm, w_row_ref, w_t_ref, stage_ref, sem_ref):
    layout, _, _ = _in_layout()
    n_rows, n_cols = w_hbm.shape
    pieces = [(r, c) for r in range(0, n_rows, IN_STAGE_ROWS) for c in range(0, n_cols, IN_STAGE_COLS)]

    def copy(k):
        r, c = pieces[k]
        return pltpu.make_async_copy(w_hbm.at[pl.ds(r, IN_STAGE_ROWS), pl.ds(c, IN_STAGE_COLS)],
                                     stage_ref.at[k % 2], sem_ref.at[k % 2])

    copy(0).start()
    for k, (r, c) in enumerate(pieces):
        if k + 1 < len(pieces):
            copy(k + 1).start()
        copy(k).wait()
        for seg0, width, transposed, dst in layout:
            lo, hi = max(seg0, c), min(seg0 + width, c + IN_STAGE_COLS)
            if lo >= hi:
                continue
            part = stage_ref[k % 2, :, lo - c:hi - c]
            d0 = dst + lo - seg0
            if transposed:
                w_t_ref[d0:d0 + hi - lo, r:r + IN_STAGE_ROWS] = part.T.astype(BF16)
            else:
                w_row_ref[r:r + IN_STAGE_ROWS, d0:d0 + hi - lo] = part.astype(BF16)


def _in_proj_kernel(x_ref, g_ref, cos_ref, sin_ref, w_hbm,
                    rq_ref, rk_ref, aq_ref, ak_ref, gl_ref, rvt_ref, rgt_ref, avt_ref,
                    w_row_ref, w_t_ref, stage_ref, sem_ref):
    @pl.when(pl.program_id(0) == 0)
    def _():
        _stage_w_in(w_hbm, w_row_ref, w_t_ref, stage_ref, sem_ref)

    x = x_ref[...]
    xn = ((x * _rms_scale(x)) * g_ref[...]).astype(BF16)
    cos = cos_ref[...]
    sin = sin_ref[...]

    def rotary_store(out_ref, col0, scale):
        t = _dot(xn, w_row_ref[:, col0:col0 + RET_QK])
        for h in range(RET_HEADS):
            cols = slice(h * RET_KEY_DIM, (h + 1) * RET_KEY_DIM)
            th = t[:, cols]
            rot = th * cos + pltpu.roll(th, RET_KEY_DIM // 2, axis=1) * sin
            out_ref[:, cols] = (rot if scale is None else rot * scale).astype(BF16)

    rotary_store(rq_ref, 0, None)
    rotary_store(rk_ref, RET_QK, RET_KEY_DIM ** -0.5)
    col = 2 * RET_QK
    for out_ref, width, scale in ((aq_ref, ATT_W, ATT_HEAD_DIM ** -0.5 * LOG2E), (ak_ref, ATT_W, None),
                                  (gl_ref, 2 * D_MODEL, 0.5)):
        for c in range(0, width, COL_CHUNK):
            t = _dot(xn, w_row_ref[:, col + c:col + c + COL_CHUNK])
            out_ref[:, c:c + COL_CHUNK] = (t if scale is None else t * scale).astype(BF16)
        col += width
    row = 0
    for out_ref, height in ((rvt_ref, RET_V), (rgt_ref, RET_V), (avt_ref, ATT_W)):
        for r in range(0, height, COL_CHUNK):
            t = _dot_nt(w_t_ref[row + r:row + r + COL_CHUNK, :], xn)
            out_ref[r:r + COL_CHUNK, :] = t.astype(BF16)
        row += height


def _in_proj(x2d, g, w_in, cos, sin, seq):
    m = x2d.shape[0]
    _, row_cols, t_rows = _in_layout()
    assert w_in.shape[0] % IN_STAGE_ROWS == 0 and w_in.shape[1] % IN_STAGE_COLS == 0
    assert IN_STAGE_ROWS % V7X_LANES == 0 and IN_STAGE_COLS % V7X_LANES == 0
    steps_per_seq = seq // IN_ROW_TILE
    row = lambda width: pl.BlockSpec((IN_ROW_TILE, width), lambda i: (i, 0))
    col = lambda height: pl.BlockSpec((height, IN_ROW_TILE), lambda i: (0, i))
    pos = pl.BlockSpec((IN_ROW_TILE, RET_KEY_DIM), lambda i: (i % steps_per_seq, 0))
    row_widths = (RET_QK, RET_QK, ATT_W, ATT_W, 2 * D_MODEL)
    col_heights = (RET_V, RET_V, ATT_W)
    return pl.pallas_call(
        _in_proj_kernel,
        grid=(m // IN_ROW_TILE,),
        in_specs=[row(D_MODEL), _const_spec(g.shape), pos, pos, pl.BlockSpec(memory_space=pl.ANY)],
        out_specs=[row(w) for w in row_widths] + [col(h) for h in col_heights],
        out_shape=([jax.ShapeDtypeStruct((m, w), BF16) for w in row_widths]
                   + [jax.ShapeDtypeStruct((h, m), BF16) for h in col_heights]),
        scratch_shapes=[pltpu.VMEM((w_in.shape[0], row_cols), BF16),
                        pltpu.VMEM((t_rows, w_in.shape[0]), BF16),
                        pltpu.VMEM((2, IN_STAGE_ROWS, IN_STAGE_COLS), F32),
                        pltpu.SemaphoreType.DMA((2,))],
        compiler_params=pltpu.CompilerParams(dimension_semantics=("arbitrary",),
                                             vmem_limit_bytes=VMEM_LIMIT_BYTES),
        name="in_proj",
    )(x2d, g, cos, sin, w_in)


def _mixer_kernel(rq_ref, rk_ref, rvt_ref, rgt_ref, aq_ref, ak_ref, avt_ref,
                  ebias_ref, cap_ref, dect_ref, qdec_ref, kdec_ref, cdec_ref,
                  rogt_ref, aot_ref, kpad_ref, vtpad_ref, state_ref, biast_ref):
    b = pl.program_id(0)
    g = pl.program_id(1)
    n_pad_blocks = PAD_ROWS // Q_BLOCK

    @pl.when((b == 0) & (g == 0))
    def _():
        for h in range(ATT_HEADS):
            ext = jnp.broadcast_to(ebias_ref[h:h + 1, :], (BIAS_ROLL_ROWS, BIAS_EXT))
            for r in range(0, KEY_WINDOW, BIAS_ROLL_ROWS):
                rolled = pltpu.roll(ext, BIAS_EXT - KEY_WINDOW + 1 + r, axis=1, stride=1, stride_axis=0)
                biast_ref[h, r:r + BIAS_ROLL_ROWS, :] = rolled[:, :Q_BLOCK]

    @pl.when(g == 0)
    def _():
        kpad_ref[0:PAD_ROWS, :] = jnp.zeros((PAD_ROWS, ATT_W), BF16)
        kpad_ref[PAD_ROWS:, :] = ak_ref[...]
        for blk in range(n_pad_blocks):
            vtpad_ref[blk] = jnp.zeros((ATT_W, Q_BLOCK), BF16)
        for blk in range(avt_ref.shape[1] // Q_BLOCK):
            vtpad_ref[n_pad_blocks + blk] = avt_ref[:, blk * Q_BLOCK:(blk + 1) * Q_BLOCK]
        state_ref[...] = jnp.zeros(state_ref.shape, F32)

    pair_w = 2 * ATT_HEAD_DIM
    low_lanes = lax.broadcasted_iota(jnp.int32, (HALF_Q, pair_w), 1) < ATT_HEAD_DIM
    first_chunk_q = lax.broadcasted_iota(jnp.int32, (CHUNK, HALF_Q), 1) < CHUNK
    no_prob = jnp.zeros((KEY_WINDOW - HALF_KEYS, HALF_Q), BF16)
    ones_rows = jnp.ones((DENOM_ROWS, KEY_WINDOW), BF16)
    half_rows = [slice(half * HALF_Q, half * HALF_Q + HALF_KEYS) for half in range(2)]

    def run_blocks(step_block0, cap_blocks):
        def attention_scores(blk, p):
            cols = slice(p * pair_w, (p + 1) * pair_w)
            start = pl.multiple_of((step_block0 + blk) * Q_BLOCK, Q_BLOCK)
            kwin = kpad_ref[pl.ds(start, KEY_WINDOW), cols]
            scores = []
            for half in range(2):
                qrows = slice(blk * Q_BLOCK + half * HALF_Q, blk * Q_BLOCK + (half + 1) * HALF_Q)
                qp = aq_ref[qrows, cols]
                zero = jnp.zeros_like(qp)
                q2 = jnp.concatenate([jnp.where(low_lanes, qp, zero), jnp.where(low_lanes, zero, qp)], axis=0)
                scores.append(_dot_nt(kwin[half_rows[half], :], q2))
            return scores

        def attention_finish(blk, p, scores):
            vwin_t = jnp.concatenate([vtpad_ref[step_block0 + blk + i] for i in range(KEY_WINDOW // Q_BLOCK)],
                                     axis=1)
            out_lanes = slice(blk * Q_BLOCK, (blk + 1) * Q_BLOCK)
            for e in range(2):
                h = 2 * p + e
                probs = []
                for half in range(2):
                    lanes = slice(half * HALF_Q, (half + 1) * HALF_Q)
                    s = scores[half][:, e * HALF_Q:(e + 1) * HALF_Q] + biast_ref[h, half_rows[half], lanes]
                    if cap_blocks[blk] is None:
                        s = jnp.concatenate([jnp.where(first_chunk_q, s[:CHUNK], NEG_INF),
                                             s[CHUNK:HALF_KEYS - CHUNK],
                                             jnp.where(first_chunk_q, NEG_INF, s[HALF_KEYS - CHUNK:])], axis=0)
                    else:
                        s = jnp.minimum(s, cap_ref[cap_blocks[blk], half_rows[half], lanes])
                    probs.append(jnp.exp2(s - jnp.max(s, axis=0, keepdims=True)).astype(BF16))
                prob_t = jnp.concatenate([jnp.concatenate([probs[0], no_prob], axis=0),
                                          jnp.concatenate([no_prob, probs[1]], axis=0)], axis=1)
                head_rows = slice(h * ATT_HEAD_DIM, (h + 1) * ATT_HEAD_DIM)
                out_t = _dot(jnp.concatenate([vwin_t[head_rows, :], ones_rows], axis=0), prob_t)
                denom = out_t[ATT_HEAD_DIM:ATT_HEAD_DIM + 1, :]
                aot_ref[head_rows, out_lanes] = (out_t[:ATT_HEAD_DIM, :] * (1.0 / denom)).astype(BF16)

        def retention_dots(blk, h):
            kcols = slice(h * RET_KEY_DIM, (h + 1) * RET_KEY_DIM)
            pos = slice(blk * Q_BLOCK, (blk + 1) * Q_BLOCK)
            q = rq_ref[pos, kcols]
            k = rk_ref[pos, kcols]
            vt = rvt_ref[h * RET_VAL_DIM:(h + 1) * RET_VAL_DIM, pos]
            state_t = state_ref[h]
            qd = (q.astype(F32) * qdec_ref[:, kcols]).astype(BF16)
            kd = (k.astype(F32) * kdec_ref[:, kcols]).astype(BF16)
            scores_t = _dot_nt(k, q)
            carried = _dot_nt(state_t.astype(BF16), qd)
            state_ref[h] = state_t * cdec_ref[h] + _dot(vt, kd)
            return scores_t, carried

        def retention_finish(blk, h, scores_t, carried):
            vrows = slice(h * RET_VAL_DIM, (h + 1) * RET_VAL_DIM)
            pos = slice(blk * Q_BLOCK, (blk + 1) * Q_BLOCK)
            out_t = _dot(rvt_ref[vrows, pos], (scores_t * dect_ref[h]).astype(BF16)) + carried
            mu = jnp.mean(out_t, axis=0, keepdims=True)
            dev = out_t - mu
            var = jnp.mean(dev * dev, axis=0, keepdims=True)
            gate = rgt_ref[vrows, pos].astype(F32)
            rogt_ref[vrows, pos] = (_silu(gate) * (dev * lax.rsqrt(var + EPS))).astype(BF16)

        assert ATT_HEADS // 2 == RET_HEADS
        items = [(blk, i) for blk in range(len(cap_blocks)) for i in range(RET_HEADS)]
        lead = lambda blk, i: (attention_scores(blk, i), retention_dots(blk, i))
        nxt = lead(*items[0])
        for idx, (blk, i) in enumerate(items):
            now = nxt
            if idx + 1 < len(items):
                nxt = lead(*items[idx + 1])
            attention_finish(blk, i, now[0])
            retention_finish(blk, i, *now[1])

    assert n_pad_blocks == STEP_BLOCKS

    @pl.when(g == 0)
    def _():
        run_blocks(0, list(range(STEP_BLOCKS)))

    @pl.when(g > 0)
    def _():
        run_blocks(g * STEP_BLOCKS, [None] * STEP_BLOCKS)


def _mixers(rq, rk, rvt, rgt, aq, ak, avt, ebias, cap, dect, qdec, kdec, cdec, batch, seq):
    m = rq.shape[0]
    q_step = STEP_BLOCKS * Q_BLOCK
    steps_per_seq = seq // q_step
    qrow = lambda width: pl.BlockSpec((q_step, width), lambda b, g: (b * steps_per_seq + g, 0))
    qcol = lambda height: pl.BlockSpec((height, q_step), lambda b, g: (0, b * steps_per_seq + g))
    return pl.pallas_call(
        _mixer_kernel,
        grid=(batch, steps_per_seq),
        in_specs=[qrow(RET_QK), qrow(RET_QK), qcol(RET_V), qcol(RET_V), qrow(ATT_W),
                  pl.BlockSpec((seq, ATT_W), lambda b, g: (b, 0)),
                  pl.BlockSpec((ATT_W, seq), lambda b, g: (0, b)),
                  _const_spec(ebias.shape), _const_spec(cap.shape), _const_spec(dect.shape),
                  _const_spec(qdec.shape), _const_spec(kdec.shape), _const_spec(cdec.shape)],
        out_specs=[qcol(RET_V), qcol(ATT_W)],
        out_shape=[jax.ShapeDtypeStruct((RET_V, m), BF16), jax.ShapeDtypeStruct((ATT_W, m), BF16)],
        scratch_shapes=[pltpu.VMEM((PAD_ROWS + seq, ATT_W), BF16),
                        pltpu.VMEM(((PAD_ROWS + seq) // Q_BLOCK, ATT_W, Q_BLOCK), BF16),
                        pltpu.VMEM((RET_HEADS, RET_VAL_DIM, RET_KEY_DIM), F32),
                        pltpu.VMEM((ATT_HEADS, KEY_WINDOW, Q_BLOCK), F32)],
        compiler_params=pltpu.CompilerParams(dimension_semantics=("arbitrary", "arbitrary"),
                                             vmem_limit_bytes=VMEM_LIMIT_BYTES),
        name="mixers",
    )(rq, rk, rvt, rgt, aq, ak, avt, ebias, cap, dect, qdec, kdec, cdec)


def _stage_weights(jobs, stage_ref, sem_ref):
    pieces = [(src, dst, r, scale) for src, dst, scale in jobs for r in range(0, dst.shape[0], STAGE_ROWS)]

    def copy(k):
        src, dst, r, _ = pieces[k]
        return pltpu.make_async_copy(src.at[pl.ds(r, STAGE_ROWS), :],
                                     stage_ref.at[k % 2, :, pl.ds(0, dst.shape[1])], sem_ref.at[k % 2])

    copy(0).start()
    for k, (_, dst, r, scale) in enumerate(pieces):
        if k + 1 < len(pieces):
            copy(k + 1).start()
        copy(k).wait()
        piece = stage_ref[k % 2, :, 0:dst.shape[1]]
        dst[r:r + STAGE_ROWS, :] = (piece if scale is None else piece * scale).astype(BF16)


def _merge_ffn_kernel(x_ref, rogt_ref, aot_ref, gl_ref, bg_ref, gf_ref, gn_ref,
                      wro_hbm, wao_hbm, wout_hbm, wg_hbm, wu_hbm, wd_hbm, o_ref,
                      wro_ref, wao_ref, wout_ref, wg_ref, wu_ref, wd_ref, stage_ref, sem_ref):
    @pl.when(pl.program_id(0) == 0)
    def _():
        _stage_weights([(wro_hbm, wro_ref, 0.5), (wao_hbm, wao_ref, 0.5), (wout_hbm, wout_ref, None),
                        (wg_hbm, wg_ref, None), (wu_hbm, wu_ref, None), (wd_hbm, wd_ref, None)],
                       stage_ref, sem_ref)

    y_ret = _dot_tn(rogt_ref[...], wro_ref[...])
    y_att = _dot_tn(aot_ref[...], wao_ref[...])
    t_ret = jnp.tanh(gl_ref[:, :D_MODEL].astype(F32) + bg_ref[:, :D_MODEL])
    t_att = jnp.tanh(gl_ref[:, D_MODEL:].astype(F32) + bg_ref[:, D_MODEL:])
    mixed = ((y_ret + t_ret * y_ret) + (y_att + t_att * y_att)).astype(BF16)
    h = x_ref[...] + _dot(mixed, wout_ref[...])
    hn = ((h * _rms_scale(h)) * gf_ref[...]).astype(BF16)
    d_ff = wg_ref.shape[1]
    ffn = None
    for c in range(0, d_ff, FFN_CHUNK):
        cols = slice(c, min(c + FFN_CHUNK, d_ff))
        gate = _dot(hn, wg_ref[:, cols])
        up = _dot(hn, wu_ref[:, cols])
        part = _dot((_silu(gate) * up).astype(BF16), wd_ref[cols, :])
        ffn = part if ffn is None else ffn + part
    h2 = h + ffn
    o_ref[...] = (h2 * _rms_scale(h2)) * gn_ref[...]


def _merge_ffn(x2d, rogt, aot, gl, b_gate, w_ret_out, w_att_out, w_out, norm_ffn, w_gate, w_up, w_down,
               norm_final):
    m = x2d.shape[0]
    weights = (w_ret_out, w_att_out, w_out, w_gate, w_up, w_down)
    assert w_gate.shape[1] % V7X_MXU_DIM == 0 and all(w.shape[0] % STAGE_ROWS == 0 for w in weights)
    row = lambda width: pl.BlockSpec((ROW_TILE, width), lambda i: (i, 0))
    col = lambda height: pl.BlockSpec((height, ROW_TILE), lambda i: (0, i))
    small = (b_gate, norm_ffn, norm_final)
    return pl.pallas_call(
        _merge_ffn_kernel,
        grid=(m // ROW_TILE,),
        in_specs=([row(D_MODEL), col(RET_V), col(ATT_W), row(2 * D_MODEL)] + [_const_spec(c.shape) for c in small]
                  + [pl.BlockSpec(memory_space=pl.ANY)] * len(weights)),
        out_specs=row(D_MODEL),
        out_shape=jax.ShapeDtypeStruct((m, D_MODEL), F32),
        scratch_shapes=([pltpu.VMEM(w.shape, BF16) for w in weights]
                        + [pltpu.VMEM((2, STAGE_ROWS, max(w.shape[1] for w in weights)), F32),
                           pltpu.SemaphoreType.DMA((2,))]),
        compiler_params=pltpu.CompilerParams(dimension_semantics=("arbitrary",),
                                             vmem_limit_bytes=VMEM_LIMIT_BYTES),
        name="merge_ffn",
    )(x2d, rogt, aot, gl, *small, *weights)


def _rotary_tables(seq):
    d = RET_KEY_DIM
    freqs = ROPE_BASE ** (-np.arange(0, d, 2, dtype=np.float64) / d)
    ang = np.arange(seq, dtype=np.float64)[:, None] * freqs[None, :]
    cos = np.cos(ang)
    sin = np.sin(ang)
    return (jnp.asarray(np.concatenate([cos, cos], axis=-1), F32),
            jnp.asarray(np.concatenate([-sin, sin], axis=-1), F32))


def _retention_tables():
    log_g = np.log(1.0 - 2.0 ** (-5.0 - np.arange(RET_HEADS, dtype=np.float64)))
    p = np.arange(Q_BLOCK, dtype=np.float64)
    dist = p[None, :] - p[:, None]
    chunk = np.arange(Q_BLOCK) // CHUNK
    same = chunk[:, None] == chunk[None, :]
    earlier = chunk[:, None] < chunk[None, :]
    decay = np.exp(log_g[:, None, None] * np.where(same, np.abs(dist), dist)[None])
    dect = np.where((same | earlier)[None], decay, 0.0)
    q_dec = np.exp(log_g[None, :] * (p[:, None] + 1.0))
    k_dec = np.exp(log_g[None, :] * (Q_BLOCK - 1.0 - p[:, None]))
    qdec = np.repeat(q_dec, RET_KEY_DIM, axis=1)
    kdec = np.repeat(k_dec, RET_KEY_DIM, axis=1)
    cdec = np.broadcast_to(np.exp(log_g * Q_BLOCK)[:, None, None], (RET_HEADS, 1, RET_KEY_DIM))
    return tuple(jnp.asarray(t, F32) for t in (dect, qdec, kdec, cdec))


def _extended_bias(rel_bias):
    left = Q_BLOCK - CHUNK
    return jnp.pad(rel_bias.astype(F32) * LOG2E, ((0, 0), (left, BIAS_EXT - left - N_REL)), mode="edge")


def _visibility_caps():
    j = np.arange(KEY_WINDOW)[:, None]
    n = np.arange(Q_BLOCK)[None, :]
    in_band = (j // CHUNK >= n // CHUNK) & (j // CHUNK <= n // CHUNK + BAND_CHUNKS)
    caps = [np.where(in_band & (g * Q_BLOCK - PAD_ROWS + j >= 0), F32_MAX, NEG_INF)
            for g in range(PAD_ROWS // Q_BLOCK)]
    return jnp.asarray(np.stack(caps), F32)


def kernel(x, norm_mix, w_in, b_gate, rel_bias, w_ret_out, w_att_out, w_out, norm_ffn,
           w_ffn_gate, w_ffn_up, w_ffn_down, norm_final):
    batch, seq, d_model = x.shape
    assert w_in.shape[0] == 1, "single-layer block"
    assert d_model == D_MODEL and seq % ROW_TILE == 0 and seq % IN_ROW_TILE == 0
    assert seq % (STEP_BLOCKS * Q_BLOCK) == 0 and PAD_ROWS == STEP_BLOCKS * Q_BLOCK
    cos, sin = _rotary_tables(seq)
    dect, qdec, kdec, cdec = _retention_tables()
    x2d = x.reshape(batch * seq, d_model)
    rq, rk, aq, ak, gl, rvt, rgt, avt = _in_proj(x2d, norm_mix, w_in[0], cos, sin, seq)
    rogt, aot = _mixers(rq, rk, rvt, rgt, aq, ak, avt, _extended_bias(rel_bias[0]), _visibility_caps(),
                        dect, qdec, kdec, cdec, batch, seq)
    out = _merge_ffn(x2d, rogt, aot, gl, 0.5 * b_gate, w_ret_out[0], w_att_out[0], w_out[0], norm_ffn,
                     w_ffn_gate[0], w_ffn_up[0], w_ffn_down[0], norm_final[None, :])
    return out.reshape(batch, seq, d_model)
```

```python
import jax
import jax.numpy as jnp
import numpy as np
from jax import lax
from jax.experimental import pallas as pl
from jax.experimental.pallas import tpu as pltpu

D_MODEL = 1024
CHUNK = 64
RET_HEADS = 4
RET_KEY_DIM = 128
RET_VAL_DIM = 256
RET_QK = RET_HEADS * RET_KEY_DIM
RET_V = RET_HEADS * RET_VAL_DIM
ATT_HEADS = 8
ATT_HEAD_DIM = 64
ATT_W = ATT_HEADS * ATT_HEAD_DIM
BAND_CHUNKS = 8
MAX_REL = 256
N_REL = CHUNK + MAX_REL
ROPE_BASE = 10000.0
EPS = 1e-6
NEG_INF = -1e30
F32_MAX = float(np.finfo(np.float32).max)

V7X_LANES = 128
V7X_MXU_DIM = 256
VMEM_LIMIT_BYTES = 60 * 1024 * 1024

IN_ROW_TILE = 1024
IN_ROW_GROUP = 512
ROW_TILE = 512
ROW_GROUP = 256
COL_CHUNK = 512
FFN_CHUNK = 6 * V7X_MXU_DIM
STAGE_ROWS = 256
IN_STAGE_ROWS = 128
IN_STAGE_COLS = 26 * V7X_LANES
Q_BLOCK = 4 * CHUNK
STEP_BLOCKS = 2
PAD_ROWS = BAND_CHUNKS * CHUNK
KEY_WINDOW = Q_BLOCK + PAD_ROWS
HALF_Q = Q_BLOCK // 2
HALF_KEYS = HALF_Q + PAD_ROWS
BIAS_EXT = 1024
BIAS_ROLL_ROWS = 128
DENOM_ROWS = 16
LOG2E = 1.4426950408889634

assert HALF_Q == V7X_LANES and KEY_WINDOW - HALF_KEYS == HALF_Q
assert BIAS_EXT >= KEY_WINDOW + Q_BLOCK - 1

BF16 = jnp.bfloat16
F32 = jnp.float32


def _const_spec(shape):
    zeros = (0,) * len(shape)
    return pl.BlockSpec(shape, lambda *_: zeros, pipeline_mode=pl.Buffered(1))


def _dot(a, b):
    return jnp.dot(a, b, preferred_element_type=F32)


def _dot_nt(a, b):
    return lax.dot_general(a, b, (((1,), (1,)), ((), ())), preferred_element_type=F32)


def _dot_tn(a, b):
    return lax.dot_general(a, b, (((0,), (0,)), ((), ())), preferred_element_type=F32)


def _silu(x):
    u = 0.5 * x
    return u + u * jnp.tanh(u)


def _rms_scale(x):
    return lax.rsqrt(jnp.mean(x * x, axis=-1, keepdims=True) + EPS)


IN_SEGMENTS = ((RET_QK, False), (RET_QK, False), (RET_V, True), (RET_V, True),
               (ATT_W, False), (ATT_W, False), (ATT_W, True), (2 * D_MODEL, False))


def _in_layout():
    layout, col, offsets = [], 0, {False: 0, True: 0}
    for width, transposed in IN_SEGMENTS:
        layout.append((col, width, transposed, offsets[transposed]))
        col += width
        offsets[transposed] += width
    return layout, offsets[False], offsets[True]


def _stage_w_in(w_hbm, w_row_ref, w_t_ref, stage_ref, sem_ref):
    layout, _, _ = _in_layout()
    n_rows, n_cols = w_hbm.shape
    pieces = [(r, c) for r in range(0, n_rows, IN_STAGE_ROWS) for c in range(0, n_cols, IN_STAGE_COLS)]

    def copy(k):
        r, c = pieces[k]
        return pltpu.make_async_copy(w_hbm.at[pl.ds(r, IN_STAGE_ROWS), pl.ds(c, IN_STAGE_COLS)],
                                     stage_ref.at[k % 2], sem_ref.at[k % 2])

    copy(0).start()
    for k, (r, c) in enumerate(pieces):
        if k + 1 < len(pieces):
            copy(k + 1).start()
        copy(k).wait()
        for seg0, width, transposed, dst in layout:
            lo, hi = max(seg0, c), min(seg0 + width, c + IN_STAGE_COLS)
            if lo >= hi:
                continue
            part = stage_ref[k % 2, :, lo - c:hi - c]
            d0 = dst + lo - seg0
            if transposed:
                w_t_ref[d0:d0 + hi - lo, r:r + IN_STAGE_ROWS] = part.T.astype(BF16)
            else:
                w_row_ref[r:r + IN_STAGE_ROWS, d0:d0 + hi - lo] = part.astype(BF16)


def _in_proj_kernel(x_ref, g_ref, cos_ref, sin_ref, w_hbm,
                    rq_ref, rk_ref, aq_ref, ak_ref, gl_ref, rvt_ref, rgt_ref, avt_ref,
                    w_row_ref, w_t_ref, stage_ref, sem_ref):
    @pl.when(pl.program_id(0) == 0)
    def _():
        _stage_w_in(w_hbm, w_row_ref, w_t_ref, stage_ref, sem_ref)

    for r0 in range(0, IN_ROW_TILE, IN_ROW_GROUP):
        rows = slice(r0, r0 + IN_ROW_GROUP)
        x = x_ref[rows, :]
        xn = ((x * _rms_scale(x)) * g_ref[...]).astype(BF16)
        cos = cos_ref[rows, :]
        sin = sin_ref[rows, :]

        def rotary_store(out_ref, col0, scale):
            t = _dot(xn, w_row_ref[:, col0:col0 + RET_QK])
            for h in range(RET_HEADS):
                cols = slice(h * RET_KEY_DIM, (h + 1) * RET_KEY_DIM)
                th = t[:, cols]
                rot = th * cos + pltpu.roll(th, RET_KEY_DIM // 2, axis=1) * sin
                out_ref[rows, cols] = (rot if scale is None else rot * scale).astype(BF16)

        rotary_store(rq_ref, 0, None)
        rotary_store(rk_ref, RET_QK, RET_KEY_DIM ** -0.5)
        col = 2 * RET_QK
        for out_ref, width, scale in ((aq_ref, ATT_W, ATT_HEAD_DIM ** -0.5 * LOG2E), (ak_ref, ATT_W, None),
                                      (gl_ref, 2 * D_MODEL, 0.5)):
            for c in range(0, width, COL_CHUNK):
                t = _dot(xn, w_row_ref[:, col + c:col + c + COL_CHUNK])
                out_ref[rows, c:c + COL_CHUNK] = (t if scale is None else t * scale).astype(BF16)
            col += width
        row = 0
        for out_ref, height in ((rvt_ref, RET_V), (rgt_ref, RET_V), (avt_ref, ATT_W)):
            for r in range(0, height, COL_CHUNK):
                t = _dot_nt(w_t_ref[row + r:row + r + COL_CHUNK, :], xn)
                out_ref[r:r + COL_CHUNK, rows] = t.astype(BF16)
            row += height


def _in_proj(x2d, g, w_in, cos, sin, seq):
    m = x2d.shape[0]
    _, row_cols, t_rows = _in_layout()
    assert w_in.shape[0] % IN_STAGE_ROWS == 0 and w_in.shape[1] % IN_STAGE_COLS == 0
    assert IN_STAGE_ROWS % V7X_LANES == 0 and IN_STAGE_COLS % V7X_LANES == 0
    steps_per_seq = seq // IN_ROW_TILE
    row = lambda width: pl.BlockSpec((IN_ROW_TILE, width), lambda i: (i, 0))
    col = lambda height: pl.BlockSpec((height, IN_ROW_TILE), lambda i: (0, i))
    pos = pl.BlockSpec((IN_ROW_TILE, RET_KEY_DIM), lambda i: (i % steps_per_seq, 0))
    row_widths = (RET_QK, RET_QK, ATT_W, ATT_W, 2 * D_MODEL)
    col_heights = (RET_V, RET_V, ATT_W)
    return pl.pallas_call(
        _in_proj_kernel,
        grid=(m // IN_ROW_TILE,),
        in_specs=[row(D_MODEL), _const_spec(g.shape), pos, pos, pl.BlockSpec(memory_space=pl.ANY)],
        out_specs=[row(w) for w in row_widths] + [col(h) for h in col_heights],
        out_shape=([jax.ShapeDtypeStruct((m, w), BF16) for w in row_widths]
                   + [jax.ShapeDtypeStruct((h, m), BF16) for h in col_heights]),
        scratch_shapes=[pltpu.VMEM((w_in.shape[0], row_cols), BF16),
                        pltpu.VMEM((t_rows, w_in.shape[0]), BF16),
                        pltpu.VMEM((2, IN_STAGE_ROWS, IN_STAGE_COLS), F32),
                        pltpu.SemaphoreType.DMA((2,))],
        compiler_params=pltpu.CompilerParams(dimension_semantics=("arbitrary",),
                                             vmem_limit_bytes=VMEM_LIMIT_BYTES),
        name="in_proj",
    )(x2d, g, cos, sin, w_in)


def _mixer_kernel(rq_ref, rk_ref, rvt_ref, rgt_ref, aq_ref, ak_ref, avt_ref,
                  ebias_ref, cap_ref, dect_ref, qdec_ref, kdec_ref, cdec_ref,
                  rogt_ref, aot_ref, kpad_ref, vtpad_ref, state_ref, biast_ref):
    b = pl.program_id(0)
    g = pl.program_id(1)
    n_pad_blocks = PAD_ROWS // Q_BLOCK

    @pl.when((b == 0) & (g == 0))
    def _():
        for h in range(ATT_HEADS):
            ext = jnp.broadcast_to(ebias_ref[h:h + 1, :], (BIAS_ROLL_ROWS, BIAS_EXT))
            for r in range(0, KEY_WINDOW, BIAS_ROLL_ROWS):
                rolled = pltpu.roll(ext, BIAS_EXT - KEY_WINDOW + 1 + r, axis=1, stride=1, stride_axis=0)
                biast_ref[h, r:r + BIAS_ROLL_ROWS, :] = rolled[:, :Q_BLOCK]

    @pl.when(g == 0)
    def _():
        kpad_ref[0:PAD_ROWS, :] = jnp.zeros((PAD_ROWS, ATT_W), BF16)
        kpad_ref[PAD_ROWS:, :] = ak_ref[...]
        for blk in range(n_pad_blocks):
            vtpad_ref[blk] = jnp.zeros((ATT_W, Q_BLOCK), BF16)
        for blk in range(avt_ref.shape[1] // Q_BLOCK):
            vtpad_ref[n_pad_blocks + blk] = avt_ref[:, blk * Q_BLOCK:(blk + 1) * Q_BLOCK]
        state_ref[...] = jnp.zeros(state_ref.shape, F32)

    pair_w = 2 * ATT_HEAD_DIM
    low_lanes = lax.broadcasted_iota(jnp.int32, (HALF_Q, pair_w), 1) < ATT_HEAD_DIM
    first_chunk_q = lax.broadcasted_iota(jnp.int32, (CHUNK, HALF_Q), 1) < CHUNK
    no_prob = jnp.zeros((KEY_WINDOW - HALF_KEYS, HALF_Q), BF16)
    ones_rows = jnp.ones((DENOM_ROWS, KEY_WINDOW), BF16)
    half_rows = [slice(half * HALF_Q, half * HALF_Q + HALF_KEYS) for half in range(2)]

    def run_blocks(step_block0, cap_blocks):
        def attention_scores(blk, p):
            cols = slice(p * pair_w, (p + 1) * pair_w)
            start = pl.multiple_of((step_block0 + blk) * Q_BLOCK, Q_BLOCK)
            kwin = kpad_ref[pl.ds(start, KEY_WINDOW), cols]
            scores = []
            for half in range(2):
                qrows = slice(blk * Q_BLOCK + half * HALF_Q, blk * Q_BLOCK + (half + 1) * HALF_Q)
                qp = aq_ref[qrows, cols]
                zero = jnp.zeros_like(qp)
                q2 = jnp.concatenate([jnp.where(low_lanes, qp, zero), jnp.where(low_lanes, zero, qp)], axis=0)
                scores.append(_dot_nt(kwin[half_rows[half], :], q2))
            return scores

        def attention_finish(blk, p, scores):
            vwin_t = jnp.concatenate([vtpad_ref[step_block0 + blk + i] for i in range(KEY_WINDOW // Q_BLOCK)],
                                     axis=1)
            out_lanes = slice(blk * Q_BLOCK, (blk + 1) * Q_BLOCK)
            for e in range(2):
                h = 2 * p + e
                probs = []
                for half in range(2):
                    lanes = slice(half * HALF_Q, (half + 1) * HALF_Q)
                    s = scores[half][:, e * HALF_Q:(e + 1) * HALF_Q] + biast_ref[h, half_rows[half], lanes]
                    if cap_blocks[blk] is None:
                        s = jnp.concatenate([jnp.where(first_chunk_q, s[:CHUNK], NEG_INF),
                                             s[CHUNK:HALF_KEYS - CHUNK],
                                             jnp.where(first_chunk_q, NEG_INF, s[HALF_KEYS - CHUNK:])], axis=0)
                    else:
                        s = jnp.minimum(s, cap_ref[cap_blocks[blk], half_rows[half], lanes])
                    probs.append(jnp.exp2(s - jnp.max(s, axis=0, keepdims=True)).astype(BF16))
                prob_t = jnp.concatenate([jnp.concatenate([probs[0], no_prob], axis=0),
                                          jnp.concatenate([no_prob, probs[1]], axis=0)], axis=1)
                head_rows = slice(h * ATT_HEAD_DIM, (h + 1) * ATT_HEAD_DIM)
                out_t = _dot(jnp.concatenate([vwin_t[head_rows, :], ones_rows], axis=0), prob_t)
                denom = out_t[ATT_HEAD_DIM:ATT_HEAD_DIM + 1, :]
                aot_ref[head_rows, out_lanes] = (out_t[:ATT_HEAD_DIM, :] * (1.0 / denom)).astype(BF16)

        def retention_dots(blk, h):
            kcols = slice(h * RET_KEY_DIM, (h + 1) * RET_KEY_DIM)
            pos = slice(blk * Q_BLOCK, (blk + 1) * Q_BLOCK)
            q = rq_ref[pos, kcols]
            k = rk_ref[pos, kcols]
            vt = rvt_ref[h * RET_VAL_DIM:(h + 1) * RET_VAL_DIM, pos]
            state_t = state_ref[h]
            qd = (q.astype(F32) * qdec_ref[:, kcols]).astype(BF16)
            kd = (k.astype(F32) * kdec_ref[:, kcols]).astype(BF16)
            scores_t = _dot_nt(k, q)
            carried = _dot_nt(state_t.astype(BF16), qd)
            state_ref[h] = state_t * cdec_ref[h] + _dot(vt, kd)
            return scores_t, carried

        def retention_finish(blk, h, scores_t, carried):
            vrows = slice(h * RET_VAL_DIM, (h + 1) * RET_VAL_DIM)
            pos = slice(blk * Q_BLOCK, (blk + 1) * Q_BLOCK)
            out_t = _dot(rvt_ref[vrows, pos], (scores_t * dect_ref[h]).astype(BF16)) + carried
            mu = jnp.mean(out_t, axis=0, keepdims=True)
            dev = out_t - mu
            var = jnp.mean(dev * dev, axis=0, keepdims=True)
            gate = rgt_ref[vrows, pos].astype(F32)
            rogt_ref[vrows, pos] = (_silu(gate) * (dev * lax.rsqrt(var + EPS))).astype(BF16)

        assert ATT_HEADS // 2 == RET_HEADS
        items = [(blk, i) for blk in range(len(cap_blocks)) for i in range(RET_HEADS)]
        lead = lambda blk, i: (attention_scores(blk, i), retention_dots(blk, i))
        nxt = lead(*items[0])
        for idx, (blk, i) in enumerate(items):
            now = nxt
            if idx + 1 < len(items):
                nxt = lead(*items[idx + 1])
            attention_finish(blk, i, now[0])
            retention_finish(blk, i, *now[1])

    assert n_pad_blocks == STEP_BLOCKS

    @pl.when(g == 0)
    def _():
        run_blocks(0, list(range(STEP_BLOCKS)))

    @pl.when(g > 0)
    def _():
        run_blocks(g * STEP_BLOCKS, [None] * STEP_BLOCKS)


def _mixers(rq, rk, rvt, rgt, aq, ak, avt, ebias, cap, dect, qdec, kdec, cdec, batch, seq):
    m = rq.shape[0]
    q_step = STEP_BLOCKS * Q_BLOCK
    steps_per_seq = seq // q_step
    qrow = lambda width: pl.BlockSpec((q_step, width), lambda b, g: (b * steps_per_seq + g, 0))
    qcol = lambda height: pl.BlockSpec((height, q_step), lambda b, g: (0, b * steps_per_seq + g))
    return pl.pallas_call(
        _mixer_kernel,
        grid=(batch, steps_per_seq),
        in_specs=[qrow(RET_QK), qrow(RET_QK), qcol(RET_V), qcol(RET_V), qrow(ATT_W),
                  pl.BlockSpec((seq, ATT_W), lambda b, g: (b, 0)),
                  pl.BlockSpec((ATT_W, seq), lambda b, g: (0, b)),
                  _const_spec(ebias.shape), _const_spec(cap.shape), _const_spec(dect.shape),
                  _const_spec(qdec.shape), _const_spec(kdec.shape), _const_spec(cdec.shape)],
        out_specs=[qcol(RET_V), qcol(ATT_W)],
        out_shape=[jax.ShapeDtypeStruct((RET_V, m), BF16), jax.ShapeDtypeStruct((ATT_W, m), BF16)],
        scratch_shapes=[pltpu.VMEM((PAD_ROWS + seq, ATT_W), BF16),
                        pltpu.VMEM(((PAD_ROWS + seq) // Q_BLOCK, ATT_W, Q_BLOCK), BF16),
                        pltpu.VMEM((RET_HEADS, RET_VAL_DIM, RET_KEY_DIM), F32),
                        pltpu.VMEM((ATT_HEADS, KEY_WINDOW, Q_BLOCK), F32)],
        compiler_params=pltpu.CompilerParams(dimension_semantics=("arbitrary", "arbitrary"),
                                             vmem_limit_bytes=VMEM_LIMIT_BYTES),
        name="mixers",
    )(rq, rk, rvt, rgt, aq, ak, avt, ebias, cap, dect, qdec, kdec, cdec)


def _stage_weights(jobs, stage_ref, sem_ref):
    pieces = [(src, dst, r, scale) for src, dst, scale in jobs for r in range(0, dst.shape[0], STAGE_ROWS)]

    def copy(k):
        src, dst, r, _ = pieces[k]
        return pltpu.make_async_copy(src.at[pl.ds(r, STAGE_ROWS), :],
                                     stage_ref.at[k % 2, :, pl.ds(0, dst.shape[1])], sem_ref.at[k % 2])

    copy(0).start()
    for k, (_, dst, r, scale) in enumerate(pieces):
        if k + 1 < len(pieces):
            copy(k + 1).start()
        copy(k).wait()
        piece = stage_ref[k % 2, :, 0:dst.shape[1]]
        dst[r:r + STAGE_ROWS, :] = (piece if scale is None else piece * scale).astype(BF16)


def _merge_ffn_kernel(x_ref, rogt_ref, aot_ref, gl_ref, bg_ref, gf_ref, gn_ref,
                      wro_hbm, wao_hbm, wout_hbm, wg_hbm, wu_hbm, wd_hbm, o_ref,
                      wro_ref, wao_ref, wout_ref, wg_ref, wu_ref, wd_ref, stage_ref, sem_ref):
    @pl.when(pl.program_id(0) == 0)
    def _():
        _stage_weights([(wro_hbm, wro_ref, 0.5), (wao_hbm, wao_ref, 0.5), (wout_hbm, wout_ref, None),
                        (wg_hbm, wg_ref, None), (wu_hbm, wu_ref, None), (wd_hbm, wd_ref, None)],
                       stage_ref, sem_ref)

    groups = [slice(r, r + ROW_GROUP) for r in range(0, ROW_TILE, ROW_GROUP)]
    merged = []
    for rows in groups:
        y_ret = _dot_tn(rogt_ref[:, rows], wro_ref[...])
        y_att = _dot_tn(aot_ref[:, rows], wao_ref[...])
        t_ret = jnp.tanh(gl_ref[rows, :D_MODEL].astype(F32) + bg_ref[:, :D_MODEL])
        t_att = jnp.tanh(gl_ref[rows, D_MODEL:].astype(F32) + bg_ref[:, D_MODEL:])
        mixed = ((y_ret + t_ret * y_ret) + (y_att + t_att * y_att)).astype(BF16)
        merged.append(x_ref[rows, :] + _dot(mixed, wout_ref[...]))
    d_ff = wg_ref.shape[1]
    for rows, h in zip(groups, merged):
        hn = ((h * _rms_scale(h)) * gf_ref[...]).astype(BF16)
        ffn = None
        for c in range(0, d_ff, FFN_CHUNK):
            cols = slice(c, min(c + FFN_CHUNK, d_ff))
            gate = _dot(hn, wg_ref[:, cols])
            up = _dot(hn, wu_ref[:, cols])
            part = _dot((_silu(gate) * up).astype(BF16), wd_ref[cols, :])
            ffn = part if ffn is None else ffn + part
        h2 = h + ffn
        o_ref[rows, :] = (h2 * _rms_scale(h2)) * gn_ref[...]


def _merge_ffn(x2d, rogt, aot, gl, b_gate, w_ret_out, w_att_out, w_out, norm_ffn, w_gate, w_up, w_down,
               norm_final):
    m = x2d.shape[0]
    weights = (w_ret_out, w_att_out, w_out, w_gate, w_up, w_down)
    assert w_gate.shape[1] % V7X_MXU_DIM == 0 and all(w.shape[0] % STAGE_ROWS == 0 for w in weights)
    row = lambda width: pl.BlockSpec((ROW_TILE, width), lambda i: (i, 0))
    col = lambda height: pl.BlockSpec((height, ROW_TILE), lambda i: (0, i))
    small = (b_gate, norm_ffn, norm_final)
    return pl.pallas_call(
        _merge_ffn_kernel,
        grid=(m // ROW_TILE,),
        in_specs=([row(D_MODEL), col(RET_V), col(ATT_W), row(2 * D_MODEL)] + [_const_spec(c.shape) for c in small]
                  + [pl.BlockSpec(memory_space=pl.ANY)] * len(weights)),
        out_specs=row(D_MODEL),
        out_shape=jax.ShapeDtypeStruct((m, D_MODEL), F32),
        scratch_shapes=([pltpu.VMEM(w.shape, BF16) for w in weights]
                        + [pltpu.VMEM((2, STAGE_ROWS, max(w.shape[1] for w in weights)), F32),
                           pltpu.SemaphoreType.DMA((2,))]),
        compiler_params=pltpu.CompilerParams(dimension_semantics=("arbitrary",),
                                             vmem_limit_bytes=VMEM_LIMIT_BYTES),
        name="merge_ffn",
    )(x2d, rogt, aot, gl, *small, *weights)


def _rotary_tables(seq):
    d = RET_KEY_DIM
    freqs = ROPE_BASE ** (-np.arange(0, d, 2, dtype=np.float64) / d)
    ang = np.arange(seq, dtype=np.float64)[:, None] * freqs[None, :]
    cos = np.cos(ang)
    sin = np.sin(ang)
    return (jnp.asarray(np.concatenate([cos, cos], axis=-1), F32),
            jnp.asarray(np.concatenate([-sin, sin], axis=-1), F32))


def _retention_tables():
    log_g = np.log(1.0 - 2.0 ** (-5.0 - np.arange(RET_HEADS, dtype=np.float64)))
    p = np.arange(Q_BLOCK, dtype=np.float64)
    dist = p[None, :] - p[:, None]
    chunk = np.arange(Q_BLOCK) // CHUNK
    same = chunk[:, None] == chunk[None, :]
    earlier = chunk[:, None] < chunk[None, :]
    decay = np.exp(log_g[:, None, None] * np.where(same, np.abs(dist), dist)[None])
    dect = np.where((same | earlier)[None], decay, 0.0)
    q_dec = np.exp(log_g[None, :] * (p[:, None] + 1.0))
    k_dec = np.exp(log_g[None, :] * (Q_BLOCK - 1.0 - p[:, None]))
    qdec = np.repeat(q_dec, RET_KEY_DIM, axis=1)
    kdec = np.repeat(k_dec, RET_KEY_DIM, axis=1)
    cdec = np.broadcast_to(np.exp(log_g * Q_BLOCK)[:, None, None], (RET_HEADS, 1, RET_KEY_DIM))
    return tuple(jnp.asarray(t, F32) for t in (dect, qdec, kdec, cdec))


def _extended_bias(rel_bias):
    left = Q_BLOCK - CHUNK
    return jnp.pad(rel_bias.astype(F32) * LOG2E, ((0, 0), (left, BIAS_EXT - left - N_REL)), mode="edge")


def _visibility_caps():
    j = np.arange(KEY_WINDOW)[:, None]
    n = np.arange(Q_BLOCK)[None, :]
    in_band = (j // CHUNK >= n // CHUNK) & (j // CHUNK <= n // CHUNK + BAND_CHUNKS)
    caps = [np.where(in_band & (g * Q_BLOCK - PAD_ROWS + j >= 0), F32_MAX, NEG_INF)
            for g in range(PAD_ROWS // Q_BLOCK)]
    return jnp.asarray(np.stack(caps), F32)


def kernel(x, norm_mix, w_in, b_gate, rel_bias, w_ret_out, w_att_out, w_out, norm_ffn,
           w_ffn_gate, w_ffn_up, w_ffn_down, norm_final):
    batch, seq, d_model = x.shape
    assert w_in.shape[0] == 1, "single-layer block"
    assert d_model == D_MODEL and seq % ROW_TILE == 0 and seq % IN_ROW_TILE == 0
    assert seq % (STEP_BLOCKS * Q_BLOCK) == 0 and PAD_ROWS == STEP_BLOCKS * Q_BLOCK
    cos, sin = _rotary_tables(seq)
    dect, qdec, kdec, cdec = _retention_tables()
    x2d = x.reshape(batch * seq, d_model)
    rq, rk, aq, ak, gl, rvt, rgt, avt = _in_proj(x2d, norm_mix, w_in[0], cos, sin, seq)
    rogt, aot = _mixers(rq, rk, rvt, rgt, aq, ak, avt, _extended_bias(rel_bias[0]), _visibility_caps(),
                        dect, qdec, kdec, cdec, batch, seq)
    out = _merge_ffn(x2d, rogt, aot, gl, 0.5 * b_gate, w_ret_out[0], w_att_out[0], w_out[0], norm_ffn,
                     w_ffn_gate[0], w_ffn_up[0], w_ffn_down[0], norm_final[None, :])
    return out.reshape(batch, seq, d_model)
```

```python
import jax
import jax.numpy as jnp
import numpy as np
from jax import lax
from jax.experimental import pallas as pl
from jax.experimental.pallas import tpu as pltpu

D_MODEL = 1024
CHUNK = 64
RET_HEADS = 4
RET_KEY_DIM = 128
RET_VAL_DIM = 256
RET_QK = RET_HEADS * RET_KEY_DIM
RET_V = RET_HEADS * RET_VAL_DIM
ATT_HEADS = 8
ATT_HEAD_DIM = 64
ATT_W = ATT_HEADS * ATT_HEAD_DIM
BAND_CHUNKS = 8
MAX_REL = 256
N_REL = CHUNK + MAX_REL
ROPE_BASE = 10000.0
EPS = 1e-6
NEG_INF = -1e30
F32_MAX = float(np.finfo(np.float32).max)

V7X_LANES = 128
V7X_MXU_DIM = 256
VMEM_LIMIT_BYTES = 60 * 1024 * 1024

IN_ROW_TILE = 1024
IN_ROW_GROUP = 512
ROW_TILE = 512
ROW_GROUP = 256
COL_CHUNK = 512
FFN_CHUNK = 6 * V7X_MXU_DIM
BF16_SUBLANES = 16
NEXT_WEIGHT_SCALES = (0.5, 0.5, None, None, None, None)
IN_STAGE_ROWS = 128
IN_STAGE_COLS = 26 * V7X_LANES
Q_BLOCK = 4 * CHUNK
STEP_BLOCKS = 2
PAD_ROWS = BAND_CHUNKS * CHUNK
KEY_WINDOW = Q_BLOCK + PAD_ROWS
HALF_Q = Q_BLOCK // 2
HALF_KEYS = HALF_Q + PAD_ROWS
BIAS_EXT = 1024
BIAS_ROLL_ROWS = 128
DENOM_ROWS = 16
LOG2E = 1.4426950408889634

assert HALF_Q == V7X_LANES and KEY_WINDOW - HALF_KEYS == HALF_Q
assert BIAS_EXT >= KEY_WINDOW + Q_BLOCK - 1

BF16 = jnp.bfloat16
F32 = jnp.float32


def _const_spec(shape):
    zeros = (0,) * len(shape)
    return pl.BlockSpec(shape, lambda *_: zeros, pipeline_mode=pl.Buffered(1))


def _dot(a, b):
    return jnp.dot(a, b, preferred_element_type=F32)


def _dot_nt(a, b):
    return lax.dot_general(a, b, (((1,), (1,)), ((), ())), preferred_element_type=F32)


def _dot_tn(a, b):
    return lax.dot_general(a, b, (((0,), (0,)), ((), ())), preferred_element_type=F32)


def _silu(x):
    u = 0.5 * x
    return u + u * jnp.tanh(u)


def _rms_scale(x):
    return lax.rsqrt(jnp.mean(x * x, axis=-1, keepdims=True) + EPS)


IN_SEGMENTS = ((RET_QK, False), (RET_QK, False), (RET_V, True), (RET_V, True),
               (ATT_W, False), (ATT_W, False), (ATT_W, True), (2 * D_MODEL, False))


def _in_layout():
    layout, col, offsets = [], 0, {False: 0, True: 0}
    for width, transposed in IN_SEGMENTS:
        layout.append((col, width, transposed, offsets[transposed]))
        col += width
        offsets[transposed] += width
    return layout, offsets[False], offsets[True]


def _stage_w_in(w_hbm, w_row_ref, w_t_ref, stage_ref, sem_ref):
    layout, _, _ = _in_layout()
    n_rows, n_cols = w_hbm.shape
    pieces = [(r, c) for r in range(0, n_rows, IN_STAGE_ROWS) for c in range(0, n_cols, IN_STAGE_COLS)]

    def copy(k):
        r, c = pieces[k]
        return pltpu.make_async_copy(w_hbm.at[pl.ds(r, IN_STAGE_ROWS), pl.ds(c, IN_STAGE_COLS)],
                                     stage_ref.at[k % 2], sem_ref.at[k % 2])

    copy(0).start()
    for k, (r, c) in enumerate(pieces):
        if k + 1 < len(pieces):
            copy(k + 1).start()
        copy(k).wait()
        for seg0, width, transposed, dst in layout:
            lo, hi = max(seg0, c), min(seg0 + width, c + IN_STAGE_COLS)
            if lo >= hi:
                continue
            part = stage_ref[k % 2, :, lo - c:hi - c]
            d0 = dst + lo - seg0
            if transposed:
                w_t_ref[d0:d0 + hi - lo, r:r + IN_STAGE_ROWS] = part.T.astype(BF16)
            else:
                w_row_ref[r:r + IN_STAGE_ROWS, d0:d0 + hi - lo] = part.astype(BF16)


def _in_proj_kernel(x_ref, g_ref, cos_ref, sin_ref, w_hbm,
                    rq_ref, rk_ref, aq_ref, ak_ref, gl_ref, rvt_ref, rgt_ref, avt_ref,
                    w_row_ref, w_t_ref, stage_ref, sem_ref):
    @pl.when(pl.program_id(0) == 0)
    def _():
        _stage_w_in(w_hbm, w_row_ref, w_t_ref, stage_ref, sem_ref)

    for r0 in range(0, IN_ROW_TILE, IN_ROW_GROUP):
        rows = slice(r0, r0 + IN_ROW_GROUP)
        x = x_ref[rows, :]
        xn = ((x * _rms_scale(x)) * g_ref[...]).astype(BF16)
        cos = cos_ref[rows, :]
        sin = sin_ref[rows, :]

        def rotary_store(out_ref, col0, scale):
            t = _dot(xn, w_row_ref[:, col0:col0 + RET_QK])
            for h in range(RET_HEADS):
                cols = slice(h * RET_KEY_DIM, (h + 1) * RET_KEY_DIM)
                th = t[:, cols]
                rot = th * cos + pltpu.roll(th, RET_KEY_DIM // 2, axis=1) * sin
                out_ref[rows, cols] = (rot if scale is None else rot * scale).astype(BF16)

        rotary_store(rq_ref, 0, None)
        rotary_store(rk_ref, RET_QK, RET_KEY_DIM ** -0.5)
        col = 2 * RET_QK
        for out_ref, width, scale in ((aq_ref, ATT_W, ATT_HEAD_DIM ** -0.5 * LOG2E), (ak_ref, ATT_W, None),
                                      (gl_ref, 2 * D_MODEL, 0.5)):
            for c in range(0, width, COL_CHUNK):
                t = _dot(xn, w_row_ref[:, col + c:col + c + COL_CHUNK])
                out_ref[rows, c:c + COL_CHUNK] = (t if scale is None else t * scale).astype(BF16)
            col += width
        row = 0
        for out_ref, height in ((rvt_ref, RET_V), (rgt_ref, RET_V), (avt_ref, ATT_W)):
            for r in range(0, height, COL_CHUNK):
                t = _dot_nt(w_t_ref[row + r:row + r + COL_CHUNK, :], xn)
                out_ref[r:r + COL_CHUNK, rows] = t.astype(BF16)
            row += height


def _in_proj(x2d, g, w_in, cos, sin, seq):
    m = x2d.shape[0]
    _, row_cols, t_rows = _in_layout()
    assert w_in.shape[0] % IN_STAGE_ROWS == 0 and w_in.shape[1] % IN_STAGE_COLS == 0
    assert IN_STAGE_ROWS % V7X_LANES == 0 and IN_STAGE_COLS % V7X_LANES == 0
    steps_per_seq = seq // IN_ROW_TILE
    row = lambda width: pl.BlockSpec((IN_ROW_TILE, width), lambda i: (i, 0))
    col = lambda height: pl.BlockSpec((height, IN_ROW_TILE), lambda i: (0, i))
    pos = pl.BlockSpec((IN_ROW_TILE, RET_KEY_DIM), lambda i: (i % steps_per_seq, 0))
    row_widths = (RET_QK, RET_QK, ATT_W, ATT_W, 2 * D_MODEL)
    col_heights = (RET_V, RET_V, ATT_W)
    return pl.pallas_call(
        _in_proj_kernel,
        grid=(m // IN_ROW_TILE,),
        in_specs=[row(D_MODEL), _const_spec(g.shape), pos, pos, pl.BlockSpec(memory_space=pl.ANY)],
        out_specs=[row(w) for w in row_widths] + [col(h) for h in col_heights],
        out_shape=([jax.ShapeDtypeStruct((m, w), BF16) for w in row_widths]
                   + [jax.ShapeDtypeStruct((h, m), BF16) for h in col_heights]),
        scratch_shapes=[pltpu.VMEM((w_in.shape[0], row_cols), BF16),
                        pltpu.VMEM((t_rows, w_in.shape[0]), BF16),
                        pltpu.VMEM((2, IN_STAGE_ROWS, IN_STAGE_COLS), F32),
                        pltpu.SemaphoreType.DMA((2,))],
        compiler_params=pltpu.CompilerParams(dimension_semantics=("arbitrary",),
                                             vmem_limit_bytes=VMEM_LIMIT_BYTES),
        name="in_proj",
    )(x2d, g, cos, sin, w_in)


def _mixer_kernel(rq_ref, rk_ref, rvt_ref, rgt_ref, aq_ref, ak_ref, avt_ref,
                  ebias_ref, cap_ref, dect_ref, qdec_ref, kdec_ref, cdec_ref, *refs):
    n_w = len(NEXT_WEIGHT_SCALES)
    w_f32_refs = refs[:n_w]
    rogt_ref, aot_ref = refs[n_w:n_w + 2]
    w_bf16_refs = refs[n_w + 2:2 * n_w + 2]
    kpad_ref, vtpad_ref, state_ref, biast_ref = refs[2 * n_w + 2:]
    b = pl.program_id(0)
    g = pl.program_id(1)
    n_pad_blocks = PAD_ROWS // Q_BLOCK

    @pl.when((b == 0) & (g == 0))
    def _():
        for h in range(ATT_HEADS):
            ext = jnp.broadcast_to(ebias_ref[h:h + 1, :], (BIAS_ROLL_ROWS, BIAS_EXT))
            for r in range(0, KEY_WINDOW, BIAS_ROLL_ROWS):
                rolled = pltpu.roll(ext, BIAS_EXT - KEY_WINDOW + 1 + r, axis=1, stride=1, stride_axis=0)
                biast_ref[h, r:r + BIAS_ROLL_ROWS, :] = rolled[:, :Q_BLOCK]

    @pl.when(g == 0)
    def _():
        kpad_ref[0:PAD_ROWS, :] = jnp.zeros((PAD_ROWS, ATT_W), BF16)
        kpad_ref[PAD_ROWS:, :] = ak_ref[...]
        for blk in range(n_pad_blocks):
            vtpad_ref[blk] = jnp.zeros((ATT_W, Q_BLOCK), BF16)
        for blk in range(avt_ref.shape[1] // Q_BLOCK):
            vtpad_ref[n_pad_blocks + blk] = avt_ref[:, blk * Q_BLOCK:(blk + 1) * Q_BLOCK]
        state_ref[...] = jnp.zeros(state_ref.shape, F32)
        for src, dst, scale in zip(w_f32_refs, w_bf16_refs, NEXT_WEIGHT_SCALES):
            dst[...] = (src[...] if scale is None else src[...] * scale).astype(BF16)

    pair_w = 2 * ATT_HEAD_DIM
    low_lanes = lax.broadcasted_iota(jnp.int32, (HALF_Q, pair_w), 1) < ATT_HEAD_DIM
    first_chunk_q = lax.broadcasted_iota(jnp.int32, (CHUNK, HALF_Q), 1) < CHUNK
    no_prob = jnp.zeros((KEY_WINDOW - HALF_KEYS, HALF_Q), BF16)
    ones_rows = jnp.ones((DENOM_ROWS, KEY_WINDOW), BF16)
    half_rows = [slice(half * HALF_Q, half * HALF_Q + HALF_KEYS) for half in range(2)]

    def run_blocks(step_block0, cap_blocks):
        def attention_scores(blk, p):
            cols = slice(p * pair_w, (p + 1) * pair_w)
            start = pl.multiple_of((step_block0 + blk) * Q_BLOCK, Q_BLOCK)
            kwin = kpad_ref[pl.ds(start, KEY_WINDOW), cols]
            scores = []
            for half in range(2):
                qrows = slice(blk * Q_BLOCK + half * HALF_Q, blk * Q_BLOCK + (half + 1) * HALF_Q)
                qp = aq_ref[qrows, cols]
                zero = jnp.zeros_like(qp)
                q2 = jnp.concatenate([jnp.where(low_lanes, qp, zero), jnp.where(low_lanes, zero, qp)], axis=0)
                scores.append(_dot_nt(kwin[half_rows[half], :], q2))
            return scores

        def attention_finish(blk, p, scores):
            vwin_t = jnp.concatenate([vtpad_ref[step_block0 + blk + i] for i in range(KEY_WINDOW // Q_BLOCK)],
                                     axis=1)
            out_lanes = slice(blk * Q_BLOCK, (blk + 1) * Q_BLOCK)
            for e in range(2):
                h = 2 * p + e
                probs = []
                for half in range(2):
                    lanes = slice(half * HALF_Q, (half + 1) * HALF_Q)
                    s = scores[half][:, e * HALF_Q:(e + 1) * HALF_Q] + biast_ref[h, half_rows[half], lanes]
                    if cap_blocks[blk] is None:
                        s = jnp.concatenate([jnp.where(first_chunk_q, s[:CHUNK], NEG_INF),
                                             s[CHUNK:HALF_KEYS - CHUNK],
                                             jnp.where(first_chunk_q, NEG_INF, s[HALF_KEYS - CHUNK:])], axis=0)
                    else:
                        s = jnp.minimum(s, cap_ref[cap_blocks[blk], half_rows[half], lanes])
                    probs.append(jnp.exp2(s - jnp.max(s, axis=0, keepdims=True)).astype(BF16))
                prob_t = jnp.concatenate([jnp.concatenate([probs[0], no_prob], axis=0),
                                          jnp.concatenate([no_prob, probs[1]], axis=0)], axis=1)
                head_rows = slice(h * ATT_HEAD_DIM, (h + 1) * ATT_HEAD_DIM)
                out_t = _dot(jnp.concatenate([vwin_t[head_rows, :], ones_rows], axis=0), prob_t)
                denom = out_t[ATT_HEAD_DIM:ATT_HEAD_DIM + 1, :]
                aot_ref[head_rows, out_lanes] = (out_t[:ATT_HEAD_DIM, :] * (1.0 / denom)).astype(BF16)

        def retention_dots(blk, h):
            kcols = slice(h * RET_KEY_DIM, (h + 1) * RET_KEY_DIM)
            pos = slice(blk * Q_BLOCK, (blk + 1) * Q_BLOCK)
            q = rq_ref[pos, kcols]
            k = rk_ref[pos, kcols]
            vt = rvt_ref[h * RET_VAL_DIM:(h + 1) * RET_VAL_DIM, pos]
            state_t = state_ref[h]
            qd = (q.astype(F32) * qdec_ref[:, kcols]).astype(BF16)
            kd = (k.astype(F32) * kdec_ref[:, kcols]).astype(BF16)
            scores_t = _dot_nt(k, q)
            carried = _dot_nt(state_t.astype(BF16), qd)
            state_ref[h] = state_t * cdec_ref[h] + _dot(vt, kd)
            return scores_t, carried

        def retention_finish(blk, h, scores_t, carried):
            vrows = slice(h * RET_VAL_DIM, (h + 1) * RET_VAL_DIM)
            pos = slice(blk * Q_BLOCK, (blk + 1) * Q_BLOCK)
            out_t = _dot(rvt_ref[vrows, pos], (scores_t * dect_ref[h]).astype(BF16)) + carried
            mu = jnp.mean(out_t, axis=0, keepdims=True)
            dev = out_t - mu
            var = jnp.mean(dev * dev, axis=0, keepdims=True)
            gate = rgt_ref[vrows, pos].astype(F32)
            rogt_ref[vrows, pos] = (_silu(gate) * (dev * lax.rsqrt(var + EPS))).astype(BF16)

        assert ATT_HEADS // 2 == RET_HEADS
        items = [(blk, i) for blk in range(len(cap_blocks)) for i in range(RET_HEADS)]
        lead = lambda blk, i: (attention_scores(blk, i), retention_dots(blk, i))
        nxt = lead(*items[0])
        for idx, (blk, i) in enumerate(items):
            now = nxt
            if idx + 1 < len(items):
                nxt = lead(*items[idx + 1])
            attention_finish(blk, i, now[0])
            retention_finish(blk, i, *now[1])

    assert n_pad_blocks <= STEP_BLOCKS

    @pl.when(g == 0)
    def _():
        run_blocks(0, [blk if blk < n_pad_blocks else None for blk in range(STEP_BLOCKS)])

    @pl.when(g > 0)
    def _():
        run_blocks(g * STEP_BLOCKS, [None] * STEP_BLOCKS)


def _mixers(rq, rk, rvt, rgt, aq, ak, avt, ebias, cap, dect, qdec, kdec, cdec, next_weights, batch, seq):
    m = rq.shape[0]
    assert len(next_weights) == len(NEXT_WEIGHT_SCALES)
    assert all(w.shape[0] % (batch * BF16_SUBLANES) == 0 for w in next_weights)
    slab = lambda w: pl.BlockSpec((w.shape[0] // batch, w.shape[1]), lambda b, g: (b, 0))
    q_step = STEP_BLOCKS * Q_BLOCK
    steps_per_seq = seq // q_step
    qrow = lambda width: pl.BlockSpec((q_step, width), lambda b, g: (b * steps_per_seq + g, 0))
    qcol = lambda height: pl.BlockSpec((height, q_step), lambda b, g: (0, b * steps_per_seq + g))
    return pl.pallas_call(
        _mixer_kernel,
        grid=(batch, steps_per_seq),
        in_specs=[qrow(RET_QK), qrow(RET_QK), qcol(RET_V), qcol(RET_V), qrow(ATT_W),
                  pl.BlockSpec((seq, ATT_W), lambda b, g: (b, 0)),
                  pl.BlockSpec((ATT_W, seq), lambda b, g: (0, b)),
                  _const_spec(ebias.shape), _const_spec(cap.shape), _const_spec(dect.shape),
                  _const_spec(qdec.shape), _const_spec(kdec.shape), _const_spec(cdec.shape)]
                 + [slab(w) for w in next_weights],
        out_specs=[qcol(RET_V), qcol(ATT_W)] + [slab(w) for w in next_weights],
        out_shape=([jax.ShapeDtypeStruct((RET_V, m), BF16), jax.ShapeDtypeStruct((ATT_W, m), BF16)]
                   + [jax.ShapeDtypeStruct(w.shape, BF16) for w in next_weights]),
        scratch_shapes=[pltpu.VMEM((PAD_ROWS + seq, ATT_W), BF16),
                        pltpu.VMEM(((PAD_ROWS + seq) // Q_BLOCK, ATT_W, Q_BLOCK), BF16),
                        pltpu.VMEM((RET_HEADS, RET_VAL_DIM, RET_KEY_DIM), F32),
                        pltpu.VMEM((ATT_HEADS, KEY_WINDOW, Q_BLOCK), F32)],
        compiler_params=pltpu.CompilerParams(dimension_semantics=("arbitrary", "arbitrary"),
                                             vmem_limit_bytes=VMEM_LIMIT_BYTES),
        name="mixers",
    )(rq, rk, rvt, rgt, aq, ak, avt, ebias, cap, dect, qdec, kdec, cdec, *next_weights)


def _merge_ffn_kernel(x_ref, rogt_ref, aot_ref, gl_ref, bg_ref, wro_ref, wao_ref, wout_ref,
                      gf_ref, wg_ref, wu_ref, wd_ref, gn_ref, o_ref):
    groups = [slice(r, r + ROW_GROUP) for r in range(0, ROW_TILE, ROW_GROUP)]
    merged = []
    for rows in groups:
        y_ret = _dot_tn(rogt_ref[:, rows], wro_ref[...])
        y_att = _dot_tn(aot_ref[:, rows], wao_ref[...])
        t_ret = jnp.tanh(gl_ref[rows, :D_MODEL].astype(F32) + bg_ref[:, :D_MODEL])
        t_att = jnp.tanh(gl_ref[rows, D_MODEL:].astype(F32) + bg_ref[:, D_MODEL:])
        mixed = ((y_ret + t_ret * y_ret) + (y_att + t_att * y_att)).astype(BF16)
        merged.append(x_ref[rows, :] + _dot(mixed, wout_ref[...]))
    d_ff = wg_ref.shape[1]
    for rows, h in zip(groups, merged):
        hn = ((h * _rms_scale(h)) * gf_ref[...]).astype(BF16)
        ffn = None
        for c in range(0, d_ff, FFN_CHUNK):
            cols = slice(c, min(c + FFN_CHUNK, d_ff))
            gate = _dot(hn, wg_ref[:, cols])
            up = _dot(hn, wu_ref[:, cols])
            part = _dot((_silu(gate) * up).astype(BF16), wd_ref[cols, :])
            ffn = part if ffn is None else ffn + part
        h2 = h + ffn
        o_ref[rows, :] = (h2 * _rms_scale(h2)) * gn_ref[...]


def _merge_ffn(x2d, rogt, aot, gl, b_gate, w_ret_out, w_att_out, w_out, norm_ffn, w_gate, w_up, w_down,
               norm_final):
    m = x2d.shape[0]
    assert w_gate.shape[1] % V7X_MXU_DIM == 0
    row = lambda width: pl.BlockSpec((ROW_TILE, width), lambda i: (i, 0))
    col = lambda height: pl.BlockSpec((height, ROW_TILE), lambda i: (0, i))
    consts = (b_gate, w_ret_out, w_att_out, w_out, norm_ffn, w_gate, w_up, w_down, norm_final)
    return pl.pallas_call(
        _merge_ffn_kernel,
        grid=(m // ROW_TILE,),
        in_specs=[row(D_MODEL), col(RET_V), col(ATT_W), row(2 * D_MODEL)] + [_const_spec(c.shape) for c in consts],
        out_specs=row(D_MODEL),
        out_shape=jax.ShapeDtypeStruct((m, D_MODEL), F32),
        compiler_params=pltpu.CompilerParams(dimension_semantics=("arbitrary",),
                                             vmem_limit_bytes=VMEM_LIMIT_BYTES),
        name="merge_ffn",
    )(x2d, rogt, aot, gl, *consts)


def _rotary_tables(seq):
    d = RET_KEY_DIM
    freqs = ROPE_BASE ** (-np.arange(0, d, 2, dtype=np.float64) / d)
    ang = np.arange(seq, dtype=np.float64)[:, None] * freqs[None, :]
    cos = np.cos(ang)
    sin = np.sin(ang)
    return (jnp.asarray(np.concatenate([cos, cos], axis=-1), F32),
            jnp.asarray(np.concatenate([-sin, sin], axis=-1), F32))


def _retention_tables():
    log_g = np.log(1.0 - 2.0 ** (-5.0 - np.arange(RET_HEADS, dtype=np.float64)))
    p = np.arange(Q_BLOCK, dtype=np.float64)
    dist = p[None, :] - p[:, None]
    chunk = np.arange(Q_BLOCK) // CHUNK
    same = chunk[:, None] == chunk[None, :]
    earlier = chunk[:, None] < chunk[None, :]
    decay = np.exp(log_g[:, None, None] * np.where(same, np.abs(dist), dist)[None])
    dect = np.where((same | earlier)[None], decay, 0.0)
    q_dec = np.exp(log_g[None, :] * (p[:, None] + 1.0))
    k_dec = np.exp(log_g[None, :] * (Q_BLOCK - 1.0 - p[:, None]))
    qdec = np.repeat(q_dec, RET_KEY_DIM, axis=1)
    kdec = np.repeat(k_dec, RET_KEY_DIM, axis=1)
    cdec = np.broadcast_to(np.exp(log_g * Q_BLOCK)[:, None, None], (RET_HEADS, 1, RET_KEY_DIM))
    return tuple(jnp.asarray(t, F32) for t in (dect, qdec, kdec, cdec))


def _extended_bias(rel_bias):
    left = Q_BLOCK - CHUNK
    return jnp.pad(rel_bias.astype(F32) * LOG2E, ((0, 0), (left, BIAS_EXT - left - N_REL)), mode="edge")


def _visibility_caps():
    j = np.arange(KEY_WINDOW)[:, None]
    n = np.arange(Q_BLOCK)[None, :]
    in_band = (j // CHUNK >= n // CHUNK) & (j // CHUNK <= n // CHUNK + BAND_CHUNKS)
    caps = [np.where(in_band & (g * Q_BLOCK - PAD_ROWS + j >= 0), F32_MAX, NEG_INF)
            for g in range(PAD_ROWS // Q_BLOCK)]
    return jnp.asarray(np.stack(caps), F32)


def kernel(x, norm_mix, w_in, b_gate, rel_bias, w_ret_out, w_att_out, w_out, norm_ffn,
           w_ffn_gate, w_ffn_up, w_ffn_down, norm_final):
    batch, seq, d_model = x.shape
    assert w_in.shape[0] == 1, "single-layer block"
    assert d_model == D_MODEL and seq % ROW_TILE == 0 and seq % IN_ROW_TILE == 0
    assert seq % (STEP_BLOCKS * Q_BLOCK) == 0 and PAD_ROWS % Q_BLOCK == 0
    cos, sin = _rotary_tables(seq)
    dect, qdec, kdec, cdec = _retention_tables()
    x2d = x.reshape(batch * seq, d_model)
    rq, rk, aq, ak, gl, rvt, rgt, avt = _in_proj(x2d, norm_mix, w_in[0], cos, sin, seq)
    next_weights = (w_ret_out[0], w_att_out[0], w_out[0], w_ffn_gate[0], w_ffn_up[0], w_ffn_down[0])
    rogt, aot, w_ro, w_ao, w_o, w_g, w_u, w_d = _mixers(
        rq, rk, rvt, rgt, aq, ak, avt, _extended_bias(rel_bias[0]), _visibility_caps(),
        dect, qdec, kdec, cdec, next_weights, batch, seq)
    out = _merge_ffn(x2d, rogt, aot, gl, 0.5 * b_gate, w_ro, w_ao, w_o, norm_ffn, w_g, w_u, w_d,
                     norm_final[None, :])
    return out.reshape(batch, seq, d_model)
```
